```python
import math
import jax
import jax.numpy as jnp
from jax import lax
import numpy as np

D_MODEL = 2048
BATCH = 4
SEQ = 2048
DEPTH = 4

GRID_W = 64
CTX_LEN = 256
N_MIXERS = 3
HEAD_DIM = 128
ROPE_THETA = 10000.0
Q_BLOCK = 128
NORM_EPS = 1e-6
MASK_VALUE = -1e30

NA_HEADS = D_MODEL // HEAD_DIM
NA_WIN_H = 8
NA_WIN_W = 16

GQA_HEADS = D_MODEL // HEAD_DIM
GQA_KV_HEADS = GQA_HEADS // 4

DIFF_HEADS = D_MODEL // (2 * HEAD_DIM)
DIFF_LAMBDA_STD = 0.1

N_EXPERTS = 32
TOP_K = 4
D_EXPERT = 3 * D_MODEL // 8
SWIGLU_LIMIT = 7.0
SWIGLU_ALPHA = 1.702
MOE_BLOCK = 128

N_NA_LAYERS = len(range(0, DEPTH, N_MIXERS))
N_GQA_LAYERS = len(range(1, DEPTH, N_MIXERS))
N_DIFF_LAYERS = len(range(2, DEPTH, N_MIXERS))

kernel_name = 'hybrid_natten_gqa_diffattn_moe_dit'


def rms_norm(x, g):
    xf = x.astype(jnp.float32)
    y = xf * lax.rsqrt(jnp.mean(xf * xf, axis=-1, keepdims=True) + NORM_EPS)
    return (y * g.astype(jnp.float32)).astype(x.dtype)


def modulate(h, shift, scale):
    return h * (1.0 + scale) + shift


def softmax_f32(s):
    return jax.nn.softmax(s.astype(jnp.float32), axis=-1)


def axial_rope_tables(n_tok, dtype):
    t = jnp.arange(n_tok)
    row = (t // GRID_W).astype(jnp.float32)
    col = (t % GRID_W).astype(jnp.float32)
    half = HEAD_DIM // 2
    inv_freq = ROPE_THETA ** (-jnp.arange(0, half, 2, dtype=jnp.float32) / half)
    ang_r = row[:, None] * inv_freq[None, :]
    ang_c = col[:, None] * inv_freq[None, :]
    ang = jnp.concatenate([ang_r, ang_r, ang_c, ang_c], axis=-1)
    return jnp.cos(ang).astype(dtype), jnp.sin(ang).astype(dtype)


def rotate_half(u):
    h = u.shape[-1] // 2
    return jnp.concatenate([-u[..., h:], u[..., :h]], axis=-1)


def apply_axial_rope(x, cos, sin):
    half = x.shape[-1] // 2
    rot = jnp.concatenate([rotate_half(x[..., :half]), rotate_half(x[..., half:])], axis=-1)
    return x * cos + rot * sin


def sweep_query_blocks(fn, q):
    B, H, S, Dh = q.shape
    qb = q.reshape(B, H, S // Q_BLOCK, Q_BLOCK, Dh).transpose(2, 0, 1, 3, 4)
    out = lax.map(fn, qb)
    nb, _, Ho, blk, Dv = out.shape
    return out.transpose(1, 2, 0, 3, 4).reshape(B, Ho, nb * blk, Dv)


def heads_out(o, w_o):
    B, H, T, Dv = o.shape
    return o.transpose(0, 2, 1, 3).reshape(B, T, H * Dv) @ w_o


def project_out(o_ctx, o_lat, w_o, need_ctx):
    if need_ctx:
        L = o_ctx.shape[2]
        y = heads_out(jnp.concatenate([o_ctx, o_lat], axis=2), w_o)
        return y[:, :L], y[:, L:]
    return None, heads_out(o_lat, w_o)


def neighbourhood_attention(h_ctx, h_lat, w_qkv, w_o, rpb, need_ctx):
    B, S, _ = h_lat.shape
    L = h_ctx.shape[1]
    rows = S // GRID_W
    wh = min(NA_WIN_H, rows)
    scale = HEAD_DIM ** -0.5
    qkv = (jnp.concatenate([h_ctx, h_lat], axis=1) @ w_qkv).reshape(B, L + S, 3, NA_HEADS, HEAD_DIM)
    q, k, v = (qkv[:, :, i].transpose(0, 2, 1, 3) for i in range(3))
    kc, vc = k[:, :, :L], v[:, :, :L]
    kg = k[:, :, L:].reshape(B, NA_HEADS, rows, GRID_W, HEAD_DIM)
    vg = v[:, :, L:].reshape(B, NA_HEADS, rows, GRID_W, HEAD_DIM)
    qg = q[:, :, L:].reshape(B, NA_HEADS, rows, GRID_W, HEAD_DIM).transpose(2, 0, 1, 3, 4)
    col = jnp.arange(GRID_W)
    col_start = jnp.clip(col - NA_WIN_W // 2, 0, GRID_W - NA_WIN_W)
    col_mask = (col[None, :] >= col_start[:, None]) & (col[None, :] < col_start[:, None] + NA_WIN_W)
    dc_idx = jnp.clip(col[None, :] - col[:, None] + NA_WIN_W - 1, 0, 2 * NA_WIN_W - 2)

    def row_block(args):
        r, q_r = args
        r0 = jnp.clip(r - wh // 2, 0, rows - wh)
        k_win = lax.dynamic_slice_in_dim(kg, r0, wh, axis=2)
        v_win = lax.dynamic_slice_in_dim(vg, r0, wh, axis=2)
        dr_idx = r0 + jnp.arange(wh) - r + NA_WIN_H - 1
        bias = rpb[:, dr_idx[:, None, None], dc_idx[None, :, :]].transpose(0, 2, 1, 3)
        s_lat = jnp.einsum('bhqd,bhrkd->bhqrk', q_r, k_win).astype(jnp.float32) * scale + bias[None].astype(jnp.float32)
        s_lat = jnp.where(col_mask[:, None, :], s_lat, MASK_VALUE)
        s_ctx = jnp.einsum('bhqd,bhkd->bhqk', q_r, kc).astype(jnp.float32) * scale
        p = softmax_f32(jnp.concatenate([s_ctx, s_lat.reshape(B, NA_HEADS, GRID_W, wh * GRID_W)], axis=-1)).astype(v_win.dtype)
        p_lat = p[..., L:].reshape(B, NA_HEADS, GRID_W, wh, GRID_W)
        return jnp.einsum('bhqk,bhkd->bhqd', p[..., :L], vc) + jnp.einsum('bhqrk,bhrkd->bhqd', p_lat, v_win)

    o_lat = lax.map(row_block, (jnp.arange(rows), qg))
    o_lat = o_lat.transpose(1, 2, 0, 3, 4).reshape(B, NA_HEADS, S, HEAD_DIM)
    o_ctx = None
    if need_ctx:
        p_c = softmax_f32(jnp.einsum('bhqd,bhkd->bhqk', q[:, :, :L], kc).astype(jnp.float32) * scale).astype(vc.dtype)
        o_ctx = jnp.einsum('bhqk,bhkd->bhqd', p_c, vc)
    return project_out(o_ctx, o_lat, w_o, need_ctx)


def gqa_attention(h_ctx, h_lat, w_qkv, w_o, q_g, k_g, cos, sin, need_ctx):
    B, S, D = h_lat.shape
    L = h_ctx.shape[1]
    T = L + S
    G = GQA_HEADS // GQA_KV_HEADS
    kv_dim = GQA_KV_HEADS * HEAD_DIM
    scale = HEAD_DIM ** -0.5
    t = jnp.concatenate([h_ctx, h_lat], axis=1) @ w_qkv
    q = rms_norm(t[..., :D].reshape(B, T, GQA_HEADS, HEAD_DIM), q_g).transpose(0, 2, 1, 3)
    k = rms_norm(t[..., D:D + kv_dim].reshape(B, T, GQA_KV_HEADS, HEAD_DIM), k_g).transpose(0, 2, 1, 3)
    v = t[..., D + kv_dim:].reshape(B, T, GQA_KV_HEADS, HEAD_DIM).transpose(0, 2, 1, 3)
    q_lat = apply_axial_rope(q[:, :, L:], cos, sin)
    k_all = jnp.concatenate([k[:, :, :L], apply_axial_rope(k[:, :, L:], cos, sin)], axis=2)

    def attend(qb, kk, vv):
        n = qb.shape[2]
        qgrp = qb.reshape(B, GQA_KV_HEADS, G, n, HEAD_DIM)
        s = jnp.einsum('bkgqd,bksd->bkgqs', qgrp, kk).astype(jnp.float32) * scale
        p = softmax_f32(s).astype(vv.dtype)
        return jnp.einsum('bkgqs,bksd->bkgqd', p, vv).reshape(B, GQA_HEADS, n, HEAD_DIM)

    o_lat = sweep_query_blocks(lambda qb: attend(qb, k_all, v), q_lat)
    o_ctx = attend(q[:, :, :L], k[:, :, :L], v[:, :, :L]) if need_ctx else None
    return project_out(o_ctx, o_lat, w_o, need_ctx)


def diff_attention(h_ctx, h_lat, w_qkv, w_o, lam, subln_g, cos, sin, layer_idx, need_ctx):
    B, S, D = h_lat.shape
    L = h_ctx.shape[1]
    T = L + S
    scale = HEAD_DIM ** -0.5
    lambda_init = 0.8 - 0.6 * math.exp(-0.3 * layer_idx)
    lam32 = lam.astype(jnp.float32)
    lam_full = jnp.exp(jnp.sum(lam32[0] * lam32[1])) - jnp.exp(jnp.sum(lam32[2] * lam32[3])) + lambda_init
    t = jnp.concatenate([h_ctx, h_lat], axis=1) @ w_qkv
    q = t[..., :D].reshape(B, T, DIFF_HEADS, 2, HEAD_DIM).transpose(0, 2, 3, 1, 4)
    k = t[..., D:2 * D].reshape(B, T, DIFF_HEADS, 2, HEAD_DIM).transpose(0, 2, 3, 1, 4)
    v = t[..., 2 * D:].reshape(B, T, DIFF_HEADS, 2 * HEAD_DIM).transpose(0, 2, 1, 3)
    q_lat = apply_axial_rope(q[:, :, :, L:], cos, sin).reshape(B, 2 * DIFF_HEADS, S, HEAD_DIM)
    k_all = jnp.concatenate([k[:, :, :, :L], apply_axial_rope(k[:, :, :, L:], cos, sin)], axis=3)

    def attend(qb, kk, vv):
        n = qb.shape[2]
        s = jnp.einsum('bhcqd,bhckd->bhcqk', qb.reshape(B, DIFF_HEADS, 2, n, HEAD_DIM), kk).astype(jnp.float32) * scale
        p = softmax_f32(s)
        a = (p[:, :, 0] - lam_full * p[:, :, 1]).astype(vv.dtype)
        return jnp.einsum('bhqk,bhkd->bhqd', a, vv)

    def finish(o):
        return rms_norm(o, subln_g) * (1.0 - lambda_init)

    o_lat = finish(sweep_query_blocks(lambda qb: attend(qb, k_all, v), q_lat))
    o_ctx = None
    if need_ctx:
        q_c = q[:, :, :, :L].reshape(B, 2 * DIFF_HEADS, L, HEAD_DIM)
        o_ctx = finish(attend(q_c, k[:, :, :, :L], v[:, :, :L]))
    return project_out(o_ctx, o_lat, w_o, need_ctx)


def moe_ffn(h, router_w, router_b, w_gate, b_gate, w_up, b_up, w_down, b_down):
    B, T, D = h.shape
    n_tok = B * T
    xf = h.reshape(n_tok, D)
    logits = (xf @ router_w + router_b).astype(jnp.float32)
    top_val, top_idx = lax.top_k(logits, TOP_K)
    gates = jax.nn.softmax(top_val, axis=-1)
    n_assign = n_tok * TOP_K
    e_flat = top_idx.reshape(-1).astype(jnp.int32)
    order = jnp.argsort(e_flat)
    e_sorted = e_flat[order]
    tok_sorted = (order // TOP_K).astype(jnp.int32)
    gate_sorted = gates.reshape(-1)[order]
    counts = jnp.zeros((N_EXPERTS,), jnp.int32).at[e_flat].add(1)
    padded = ((counts + MOE_BLOCK - 1) // MOE_BLOCK) * MOE_BLOCK
    start = jnp.cumsum(counts) - counts
    pend = jnp.cumsum(padded)
    pstart = pend - padded
    dest = pstart[e_sorted] + (jnp.arange(n_assign, dtype=jnp.int32) - start[e_sorted])
    n_blocks = -(-n_assign // MOE_BLOCK) + N_EXPERTS
    n_slots = n_blocks * MOE_BLOCK
    slot_tok = jnp.full((n_slots,), n_tok, jnp.int32).at[dest].set(tok_sorted)
    slot_gate = jnp.zeros((n_slots,), jnp.float32).at[dest].set(gate_sorted)
    block_expert = jnp.clip(jnp.searchsorted(pend, jnp.arange(n_blocks, dtype=jnp.int32) * MOE_BLOCK, side='right'), 0, N_EXPERTS - 1)
    x_pad = jnp.concatenate([xf, jnp.zeros((1, D), xf.dtype)], axis=0)
    x_slots = x_pad[slot_tok].reshape(n_blocks, MOE_BLOCK, D)

    def expert_block(args):
        e, xb = args
        g = jnp.minimum(xb @ w_gate[e] + b_gate[e], SWIGLU_LIMIT)
        u = jnp.clip(xb @ w_up[e] + b_up[e], -SWIGLU_LIMIT, SWIGLU_LIMIT)
        return ((u + 1.0) * (g * jax.nn.sigmoid(SWIGLU_ALPHA * g))) @ w_down[e] + b_down[e]

    y_slots = lax.map(expert_block, (block_expert, x_slots)).reshape(n_slots, D)
    y = jax.ops.segment_sum(y_slots * slot_gate[:, None].astype(y_slots.dtype), slot_tok, num_segments=n_tok + 1)[:n_tok]
    return y.reshape(B, T, D)


def setup_inputs(seed: int = 0) -> dict:
    key = jax.random.key(seed)
    ks = iter(jax.random.split(key, 32))
    D = D_MODEL
    kv_dim = GQA_KV_HEADS * HEAD_DIM

    def nrm(shape, scale):
        return jax.random.normal(next(ks), shape, jnp.float32) * scale

    return {
        'x': nrm((BATCH, SEQ, D), 1.0),
        'c': nrm((BATCH, D), 1.0),
        'ctx': nrm((BATCH, CTX_LEN, D), 1.0),
        'c_ctx': nrm((D,), 1.0),
        'ada_w': nrm((DEPTH, D, 6 * D), 0.5 * D ** -0.5),
        'ada_b': nrm((DEPTH, 6 * D), 0.02),
        'norm_mix_g': 1.0 + nrm((DEPTH, D), 0.02),
        'norm_ffn_g': 1.0 + nrm((DEPTH, D), 0.02),
        'final_g': 1.0 + nrm((D,), 0.02),
        'na_wqkv': nrm((N_NA_LAYERS, D, 3 * D), D ** -0.5),
        'na_wo': nrm((N_NA_LAYERS, D, D), D ** -0.5),
        'na_rpb': nrm((N_NA_LAYERS, NA_HEADS, 2 * NA_WIN_H - 1, 2 * NA_WIN_W - 1), 0.1),
        'gqa_wqkv': nrm((N_GQA_LAYERS, D, D + 2 * kv_dim), D ** -0.5),
        'gqa_wo': nrm((N_GQA_LAYERS, D, D), D ** -0.5),
        'gqa_q_g': 1.0 + nrm((N_GQA_LAYERS, HEAD_DIM), 0.02),
        'gqa_k_g': 1.0 + nrm((N_GQA_LAYERS, HEAD_DIM), 0.02),
        'diff_wqkv': nrm((N_DIFF_LAYERS, D, 3 * D), D ** -0.5),
        'diff_wo': nrm((N_DIFF_LAYERS, D, D), D ** -0.5),
        'diff_lambda': nrm((N_DIFF_LAYERS, 4, HEAD_DIM), DIFF_LAMBDA_STD),
        'diff_subln_g': 1.0 + nrm((N_DIFF_LAYERS, 2 * HEAD_DIM), 0.02),
        'router_w': nrm((DEPTH, D, N_EXPERTS), D ** -0.5),
        'router_b': nrm((DEPTH, N_EXPERTS), 0.01),
        'w_gate': nrm((DEPTH, N_EXPERTS, D, D_EXPERT), D ** -0.5),
        'b_gate': nrm((DEPTH, N_EXPERTS, D_EXPERT), 0.02),
        'w_up': nrm((DEPTH, N_EXPERTS, D, D_EXPERT), D ** -0.5),
        'b_up': nrm((DEPTH, N_EXPERTS, D_EXPERT), 0.02),
        'w_down': nrm((DEPTH, N_EXPERTS, D_EXPERT, D), D_EXPERT ** -0.5),
        'b_down': nrm((DEPTH, N_EXPERTS, D), 0.02),
    }


def reference(x, c, ctx, c_ctx, ada_w, ada_b, norm_mix_g, norm_ffn_g, final_g,
              na_wqkv, na_wo, na_rpb, gqa_wqkv, gqa_wo, gqa_q_g, gqa_k_g,
              diff_wqkv, diff_wo, diff_lambda, diff_subln_g,
              router_w, router_b, w_gate, b_gate, w_up, b_up, w_down, b_down):
    B, S, D = x.shape
    L = ctx.shape[1]
    cos, sin = axial_rope_tables(S, x.dtype)
    silu_c = jax.nn.silu(c)
    silu_cc = jax.nn.silu(c_ctx)
    for i in range(DEPTH):
        need_ctx = i < DEPTH - 1
        mod_lat = jnp.split((silu_c @ ada_w[i] + ada_b[i])[:, None, :], 6, axis=-1)
        mod_ctx = jnp.split((silu_cc @ ada_w[i] + ada_b[i])[None, None, :], 6, axis=-1)
        h_lat = modulate(rms_norm(x, norm_mix_g[i]), mod_lat[0], mod_lat[1])
        h_ctx = modulate(rms_norm(ctx, norm_mix_g[i]), mod_ctx[0], mod_ctx[1])
        kind, j = i % N_MIXERS, i // N_MIXERS
        if kind == 0:
            y_ctx, y_lat = neighbourhood_attention(h_ctx, h_lat, na_wqkv[j], na_wo[j], na_rpb[j], need_ctx)
        elif kind == 1:
            y_ctx, y_lat = gqa_attention(h_ctx, h_lat, gqa_wqkv[j], gqa_wo[j], gqa_q_g[j], gqa_k_g[j], cos, sin, need_ctx)
        else:
            y_ctx, y_lat = diff_attention(h_ctx, h_lat, diff_wqkv[j], diff_wo[j], diff_lambda[j], diff_subln_g[j], cos, sin, i, need_ctx)
        x = x + mod_lat[2] * y_lat
        h_lat = modulate(rms_norm(x, norm_ffn_g[i]), mod_lat[3], mod_lat[4])
        moe_params = (router_w[i], router_b[i], w_gate[i], b_gate[i], w_up[i], b_up[i], w_down[i], b_down[i])
        if need_ctx:
            ctx = ctx + mod_ctx[2] * y_ctx
            h_ctx = modulate(rms_norm(ctx, norm_ffn_g[i]), mod_ctx[3], mod_ctx[4])
            y = moe_ffn(jnp.concatenate([h_ctx, h_lat], axis=1), *moe_params)
            ctx = ctx + mod_ctx[5] * y[:, :L]
            x = x + mod_lat[5] * y[:, L:]
        else:
            x = x + mod_lat[5] * moe_ffn(h_lat, *moe_params)
    return rms_norm(x, final_g)
```

```python
import functools
import math

import jax
import jax.numpy as jnp
from jax import lax
from jax.experimental import pallas as pl
from jax.experimental.pallas import tpu as pltpu

F32 = jnp.float32
BF16 = jnp.bfloat16

D_MODEL = 2048
DEPTH = 4
GRID_W = 64
CTX_LEN = 256
HEAD_DIM = 128
ROPE_THETA = 10000.0
NORM_EPS = 1e-6
MASK_VALUE = -1e30
NA_HEADS = 16
NA_WIN_H = 8
NA_WIN_W = 16
GQA_HEADS = 16
GQA_KV_HEADS = 4
DIFF_HEADS = 8
N_EXPERTS = 32
TOP_K = 4
D_EXPERT = 768
SWIGLU_LIMIT = 7.0
SWIGLU_ALPHA = 1.702

LANES = 128
ROW_TILE = 768
EXPERT_TILE = 256
MOD_ROWS = 8
VMEM_LIMIT = 56 * 1024 * 1024

_NT = (((1,), (1,)), ((), ()))


def _cparams(n_axes, vmem=VMEM_LIMIT):
    return pltpu.CompilerParams(dimension_semantics=("arbitrary",) * n_axes, vmem_limit_bytes=vmem)


def _row_mod(tile, tiles_per_batch, ctx_row, ref):
    b = tile // tiles_per_batch
    rows = lax.broadcasted_iota(jnp.int32, (ROW_TILE, 1), 0)
    is_ctx = jnp.logical_and(tile % tiles_per_batch == 0, rows < CTX_LEN)
    return jnp.where(is_ctx, ref[ctx_row:ctx_row + 1, :], ref[pl.ds(b, 1), :])


def _rms(x, g):
    return x * lax.rsqrt(jnp.mean(x * x, axis=-1, keepdims=True) + NORM_EPS) * g


def _softmax_parts(s):
    m = jnp.max(s, axis=-1, keepdims=True)
    p = jnp.exp(s - m)
    return p, jnp.sum(p, axis=-1, keepdims=True)


def _ada_kernel(c_ref, w_ref, b_ref, o_ref):
    c = c_ref[...]
    s = c * jax.nn.sigmoid(c)
    o_ref[0] = jnp.dot(s.astype(BF16), w_ref[0].astype(BF16), preferred_element_type=F32) + b_ref[0]


def _ada_mod(cvec, ada_w, ada_b):
    depth, d, n = ada_w.shape
    tn = 1024
    return pl.pallas_call(
        _ada_kernel,
        grid=(depth, n // tn),
        in_specs=[
            pl.BlockSpec((MOD_ROWS, d), lambda l, j: (0, 0)),
            pl.BlockSpec((1, d, tn), lambda l, j: (l, 0, j)),
            pl.BlockSpec((1, 1, tn), lambda l, j: (l, 0, j)),
        ],
        out_specs=pl.BlockSpec((1, MOD_ROWS, tn), lambda l, j: (l, 0, j)),
        out_shape=jax.ShapeDtypeStruct((depth, MOD_ROWS, n), F32),
        compiler_params=_cparams(2),
        name="ada_mod",
    )(cvec, ada_w, ada_b.reshape(depth, 1, n))


def _norm_mm_kernel(x_ref, g_ref, sh_ref, sc_ref, w_ref, o_ref, h_ref, *, tpb, ctx_row):
    i = pl.program_id(0)

    @pl.when(pl.program_id(1) == 0)
    def _():
        y = _rms(x_ref[...], g_ref[...])
        sh = _row_mod(i, tpb, ctx_row, sh_ref)
        sc = _row_mod(i, tpb, ctx_row, sc_ref)
        h_ref[...] = (y * (1.0 + sc) + sh).astype(BF16)

    o_ref[...] = jnp.dot(h_ref[...], w_ref[...], preferred_element_type=F32).astype(BF16)


def _norm_mm(xs, g, mod, w_bf16, tpb, ctx_row):
    n_tok, d = xs.shape
    n_out = w_bf16.shape[1]
    tn = 1024
    kern = functools.partial(_norm_mm_kernel, tpb=tpb, ctx_row=ctx_row)
    return pl.pallas_call(
        kern,
        grid=(n_tok // ROW_TILE, n_out // tn),
        in_specs=[
            pl.BlockSpec((ROW_TILE, d), lambda i, j: (i, 0)),
            pl.BlockSpec((1, d), lambda i, j: (0, 0)),
            pl.BlockSpec((MOD_ROWS, d), lambda i, j: (0, 0)),
            pl.BlockSpec((MOD_ROWS, d), lambda i, j: (0, 1)),
            pl.BlockSpec((d, tn), lambda i, j: (0, j)),
        ],
        out_specs=pl.BlockSpec((ROW_TILE, tn), lambda i, j: (i, j)),
        out_shape=jax.ShapeDtypeStruct((n_tok, n_out), BF16),
        scratch_shapes=[pltpu.VMEM((ROW_TILE, d), BF16)],
        compiler_params=_cparams(2),
        name="norm_qkv",
    )(xs, g.reshape(1, d), mod, mod, w_bf16)


def _out_proj_kernel(o_ref, w_ref, x_ref, gate_ref, y_ref, *, tpb, ctx_row):
    i = pl.program_id(0)
    gate = _row_mod(i, tpb, ctx_row, gate_ref)
    y = jnp.dot(o_ref[...], w_ref[...], preferred_element_type=F32)
    y_ref[...] = x_ref[...] + gate * y


def _out_proj(o, w_bf16, xs, mod, tpb, ctx_row):
    n_tok, d = xs.shape
    tn = 1024
    kern = functools.partial(_out_proj_kernel, tpb=tpb, ctx_row=ctx_row)
    return pl.pallas_call(
        kern,
        grid=(n_tok // ROW_TILE, d // tn),
        in_specs=[
            pl.BlockSpec((ROW_TILE, d), lambda i, j: (i, 0)),
            pl.BlockSpec((d, tn), lambda i, j: (0, j)),
            pl.BlockSpec((ROW_TILE, tn), lambda i, j: (i, j)),
            pl.BlockSpec((MOD_ROWS, tn), lambda i, j: (0, 2 * (D_MODEL // tn) + j)),
        ],
        out_specs=pl.BlockSpec((ROW_TILE, tn), lambda i, j: (i, j)),
        out_shape=jax.ShapeDtypeStruct((n_tok, d), F32),
        compiler_params=_cparams(2),
        name="out_proj",
    )(o, w_bf16, xs, mod)


def _na_kernel(q_ref, k_ref, v_ref, tab_ref, o_ref, *, need_ctx, n_lat):
    scale = HEAD_DIM ** -0.5
    L = CTX_LEN
    kc = k_ref[0, :L, :]
    vc = v_ref[0, :L, :]
    if need_ctx:
        s = lax.dot_general(q_ref[0, :L, :], kc, _NT, preferred_element_type=F32) * scale
        p, l = _softmax_parts(s)
        o = jnp.dot(p.astype(BF16), vc, preferred_element_type=F32) / l
        o_ref[0, :L, :] = o.astype(BF16)
    else:
        o_ref[0, :L, :] = jnp.zeros((L, HEAD_DIM), BF16)

    rows = n_lat // GRID_W
    qb_rows = 2
    win_rows = 10
    n_blocks = rows // qb_rows

    def body(mq, carry):
        q0 = pl.multiple_of(L + mq * (qb_rows * GRID_W), LANES)
        q = q_ref[0, pl.ds(q0, qb_rows * GRID_W), :]
        u0 = jnp.clip(qb_rows * mq - NA_WIN_H // 2, 0, rows - win_rows)
        k0 = pl.multiple_of(L + u0 * GRID_W, LANES)
        kw = k_ref[0, pl.ds(k0, win_rows * GRID_W), :]
        vw = v_ref[0, pl.ds(k0, win_rows * GRID_W), :]
        s_c = lax.dot_general(q, kc, _NT, preferred_element_type=F32) * scale
        s_l = lax.dot_general(q, kw, _NT, preferred_element_type=F32) * scale
        bias_rows = []
        for a in range(qb_rows):
            qr = qb_rows * mq + a
            r0q = jnp.clip(qr - NA_WIN_H // 2, 0, rows - NA_WIN_H)
            tiles = []
            for p in range(win_rows // 2):
                kr0 = u0 + 2 * p
                kr1 = kr0 + 1
                i0 = jnp.where(jnp.logical_and(kr0 >= r0q, kr0 < r0q + NA_WIN_H), kr0 - qr + NA_WIN_H - 1, 2 * NA_WIN_H - 1)
                i1 = jnp.where(jnp.logical_and(kr1 >= r0q, kr1 < r0q + NA_WIN_H), kr1 - qr + NA_WIN_H - 1, 2 * NA_WIN_H - 1)
                tiles.append(tab_ref[0, 0, i0] + tab_ref[0, 1, i1])
            bias_rows.append(jnp.concatenate(tiles, axis=1))
        s_l = s_l + jnp.concatenate(bias_rows, axis=0)
        m = jnp.maximum(jnp.max(s_c, axis=-1, keepdims=True), jnp.max(s_l, axis=-1, keepdims=True))
        p_c = jnp.exp(s_c - m)
        p_l = jnp.exp(s_l - m)
        l = jnp.sum(p_c, axis=-1, keepdims=True) + jnp.sum(p_l, axis=-1, keepdims=True)
        o = jnp.dot(p_c.astype(BF16), vc, preferred_element_type=F32) + jnp.dot(p_l.astype(BF16), vw, preferred_element_type=F32)
        o_ref[0, pl.ds(q0, qb_rows * GRID_W), :] = (o / l).astype(BF16)
        return carry

    lax.fori_loop(0, n_blocks, body, 0)


def _na_bias_table(rpb):
    col = jnp.arange(GRID_W)
    col_start = jnp.clip(col - NA_WIN_W // 2, 0, GRID_W - NA_WIN_W)
    col_mask = (col[None, :] >= col_start[:, None]) & (col[None, :] < col_start[:, None] + NA_WIN_W)
    dc_idx = jnp.clip(col[None, :] - col[:, None] + NA_WIN_W - 1, 0, 2 * NA_WIN_W - 2)
    tab = rpb[:, :, dc_idx]
    tab = jnp.where(col_mask[None, None], tab, MASK_VALUE)
    tab = jnp.concatenate([tab, jnp.full_like(tab[:, :1], MASK_VALUE)], axis=1)
    zeros = jnp.zeros_like(tab)
    left = jnp.concatenate([tab, zeros], axis=-1)
    right = jnp.concatenate([zeros, tab], axis=-1)
    return jnp.stack([left, right], axis=1).astype(F32)


def _na_attention(qkv, tab, need_ctx):
    b, t, _ = qkv.shape
    h = NA_HEADS
    kern = functools.partial(_na_kernel, need_ctx=need_ctx, n_lat=t - CTX_LEN)
    return pl.pallas_call(
        kern,
        grid=(b, h),
        in_specs=[
            pl.BlockSpec((1, t, HEAD_DIM), lambda bi, hi: (bi, 0, hi)),
            pl.BlockSpec((1, t, HEAD_DIM), lambda bi, hi: (bi, 0, h + hi)),
            pl.BlockSpec((1, t, HEAD_DIM), lambda bi, hi: (bi, 0, 2 * h + hi)),
            pl.BlockSpec((1, 2, 2 * NA_WIN_H, GRID_W, LANES), lambda bi, hi: (hi, 0, 0, 0, 0)),
        ],
        out_specs=pl.BlockSpec((1, t, HEAD_DIM), lambda bi, hi: (bi, 0, hi)),
        out_shape=jax.ShapeDtypeStruct((b, t, h * HEAD_DIM), BF16),
        compiler_params=_cparams(2),
        name="na_attn",
    )(qkv, qkv, qkv, tab)


def _rope_tables(n_tok):
    t = jnp.arange(n_tok)
    row = (t // GRID_W).astype(F32)
    col = (t % GRID_W).astype(F32)
    half = HEAD_DIM // 2
    inv_freq = ROPE_THETA ** (-jnp.arange(0, half, 2, dtype=F32) / half)
    ang_r = row[:, None] * inv_freq[None, :]
    ang_c = col[:, None] * inv_freq[None, :]
    ang = jnp.concatenate([ang_r, ang_r, ang_c, ang_c], axis=-1)
    cos, sin = jnp.cos(ang), jnp.sin(ang)
    upper = (jnp.arange(HEAD_DIM) % half) >= (half // 2)
    sin_a = jnp.where(upper[None, :], sin, 0.0)
    sin_b = jnp.where(upper[None, :], 0.0, -sin)
    return cos, sin_a, sin_b


def _rope(x, cos, sin_a, sin_b):
    quarter = HEAD_DIM // 4
    return x * cos + pltpu.roll(x, quarter, 1) * sin_a + pltpu.roll(x, HEAD_DIM - quarter, 1) * sin_b


def _gqa_kernel(q_ref, k_ref, v_ref, cos_ref, sa_ref, sb_ref, qg_ref, kg_ref, o_ref, kn_ref, *, need_ctx, n_lat):
    L = CTX_LEN
    G = GQA_HEADS // GQA_KV_HEADS
    scale = HEAD_DIM ** -0.5
    tq = 128

    kg = kg_ref[...]
    kn_ref[:L, :] = _rms(k_ref[0, :L, :].astype(F32), kg).astype(BF16)
    k_lat = _rms(k_ref[0, L:, :].astype(F32), kg)
    kn_ref[L:, :] = _rope(k_lat, cos_ref[...], sa_ref[...], sb_ref[...]).astype(BF16)

    qg = qg_ref[...] * scale

    if need_ctx:
        qs = [_rms(q_ref[0, :L, hh * HEAD_DIM:(hh + 1) * HEAD_DIM].astype(F32), qg).astype(BF16) for hh in range(G)]
        s = lax.dot_general(jnp.concatenate(qs, axis=0), kn_ref[:L, :], _NT, preferred_element_type=F32)
        p, l = _softmax_parts(s)
        o = jnp.dot(p.astype(BF16), v_ref[0, :L, :], preferred_element_type=F32) / l
        for hh in range(G):
            o_ref[0, :L, hh * HEAD_DIM:(hh + 1) * HEAD_DIM] = o[hh * L:(hh + 1) * L].astype(BF16)
    else:
        o_ref[0, :L, :] = jnp.zeros((L, G * HEAD_DIM), BF16)

    def body(mq, carry):
        r0 = pl.multiple_of(mq * tq, tq)
        q0 = pl.multiple_of(L + mq * tq, tq)
        cos = cos_ref[pl.ds(r0, tq), :]
        sa = sa_ref[pl.ds(r0, tq), :]
        sb = sb_ref[pl.ds(r0, tq), :]
        qs = []
        for hh in range(G):
            qn = _rms(q_ref[0, pl.ds(q0, tq), hh * HEAD_DIM:(hh + 1) * HEAD_DIM].astype(F32), qg)
            qs.append(_rope(qn, cos, sa, sb).astype(BF16))
        s = lax.dot_general(jnp.concatenate(qs, axis=0), kn_ref[...], _NT, preferred_element_type=F32)
        p, l = _softmax_parts(s)
        o = jnp.dot(p.astype(BF16), v_ref[0], preferred_element_type=F32) / l
        for hh in range(G):
            o_ref[0, pl.ds(q0, tq), hh * HEAD_DIM:(hh + 1) * HEAD_DIM] = o[hh * tq:(hh + 1) * tq].astype(BF16)
        return carry

    lax.fori_loop(0, n_lat // tq, body, 0)


def _gqa_attention(qkv, rope, q_g, k_g, need_ctx):
    b, t, _ = qkv.shape
    n_lat = t - CTX_LEN
    G = GQA_HEADS // GQA_KV_HEADS
    gw = G * HEAD_DIM
    cos, sin_a, sin_b = rope
    kern = functools.partial(_gqa_kernel, need_ctx=need_ctx, n_lat=n_lat)
    tab_spec = pl.BlockSpec((n_lat, HEAD_DIM), lambda bi, gi: (0, 0))
    vec_spec = pl.BlockSpec((1, HEAD_DIM), lambda bi, gi: (0, 0))
    return pl.pallas_call(
        kern,
        grid=(b, GQA_KV_HEADS),
        in_specs=[
            pl.BlockSpec((1, t, gw), lambda bi, gi: (bi, 0, gi)),
            pl.BlockSpec((1, t, HEAD_DIM), lambda bi, gi: (bi, 0, GQA_HEADS + gi)),
            pl.BlockSpec((1, t, HEAD_DIM), lambda bi, gi: (bi, 0, GQA_HEADS + GQA_KV_HEADS + gi)),
            tab_spec, tab_spec, tab_spec, vec_spec, vec_spec,
        ],
        out_specs=pl.BlockSpec((1, t, gw), lambda bi, gi: (bi, 0, gi)),
        out_shape=jax.ShapeDtypeStruct((b, t, GQA_HEADS * HEAD_DIM), BF16),
        scratch_shapes=[pltpu.VMEM((t, HEAD_DIM), BF16)],
        compiler_params=_cparams(2),
        name="gqa_attn",
    )(qkv, qkv, qkv, cos, sin_a, sin_b, q_g.reshape(1, HEAD_DIM), k_g.reshape(1, HEAD_DIM))


def _diff_kernel(q_ref, k_ref, v_ref, cos_ref, sa_ref, sb_ref, lam_ref, g_ref, o_ref, kr_ref, *, need_ctx, n_lat, lambda_init):
    L = CTX_LEN
    scale = HEAD_DIM ** -0.5
    tq = 128
    lam = lam_ref[...]
    lam_full = (jnp.exp(jnp.sum(lam[0:1] * lam[1:2], axis=-1, keepdims=True))
                - jnp.exp(jnp.sum(lam[2:3] * lam[3:4], axis=-1, keepdims=True)) + lambda_init)
    g = g_ref[...] * (1.0 - lambda_init)

    kr_ref[:L, :] = k_ref[0, :L, :]
    for c in range(2):
        k_lat = k_ref[0, L:, c * HEAD_DIM:(c + 1) * HEAD_DIM].astype(F32)
        kr_ref[L:, c * HEAD_DIM:(c + 1) * HEAD_DIM] = _rope(k_lat, cos_ref[...], sa_ref[...], sb_ref[...]).astype(BF16)

    def attend(qs, n_keys):
        ps = []
        for c in range(2):
            s = lax.dot_general(qs[c], kr_ref[:n_keys, c * HEAD_DIM:(c + 1) * HEAD_DIM], _NT, preferred_element_type=F32)
            p, l = _softmax_parts(s)
            ps.append(p * ((1.0 if c == 0 else lam_full) / l))
        a = (ps[0] - ps[1]).astype(BF16)
        o = jnp.dot(a, v_ref[0, :n_keys, :], preferred_element_type=F32)
        return _rms(o, g).astype(BF16)

    if need_ctx:
        qs = [(q_ref[0, :L, c * HEAD_DIM:(c + 1) * HEAD_DIM].astype(F32) * scale).astype(BF16) for c in range(2)]
        o_ref[0, :L, :] = attend(qs, L)
    else:
        o_ref[0, :L, :] = jnp.zeros((L, 2 * HEAD_DIM), BF16)

    def body(mq, carry):
        r0 = pl.multiple_of(mq * tq, tq)
        q0 = pl.multiple_of(L + mq * tq, tq)
        cos = cos_ref[pl.ds(r0, tq), :]
        sa = sa_ref[pl.ds(r0, tq), :]
        sb = sb_ref[pl.ds(r0, tq), :]
        qs = []
        for c in range(2):
            q = q_ref[0, pl.ds(q0, tq), c * HEAD_DIM:(c + 1) * HEAD_DIM].astype(F32) * scale
            qs.append(_rope(q, cos, sa, sb).astype(BF16))
        o_ref[0, pl.ds(q0, tq), :] = attend(qs, L + n_lat)
        return carry

    lax.fori_loop(0, n_lat // tq, body, 0)


def _diff_attention(qkv, rope, lam, subln_g, layer_idx, need_ctx):
    b, t, _ = qkv.shape
    n_lat = t - CTX_LEN
    hw = 2 * HEAD_DIM
    cos, sin_a, sin_b = rope
    lambda_init = 0.8 - 0.6 * math.exp(-0.3 * layer_idx)
    kern = functools.partial(_diff_kernel, need_ctx=need_ctx, n_lat=n_lat, lambda_init=lambda_init)
    tab_spec = pl.BlockSpec((n_lat, HEAD_DIM), lambda bi, hi: (0, 0))
    return pl.pallas_call(
        kern,
        grid=(b, DIFF_HEADS),
        in_specs=[
            pl.BlockSpec((1, t, hw), lambda bi, hi: (bi, 0, hi)),
            pl.BlockSpec((1, t, hw), lambda bi, hi: (bi, 0, DIFF_HEADS + hi)),
            pl.BlockSpec((1, t, hw), lambda bi, hi: (bi, 0, 2 * DIFF_HEADS + hi)),
            tab_spec, tab_spec, tab_spec,
            pl.BlockSpec((4, HEAD_DIM), lambda bi, hi: (0, 0)),
            pl.BlockSpec((1, hw), lambda bi, hi: (0, 0)),
        ],
        out_specs=pl.BlockSpec((1, t, hw), lambda bi, hi: (bi, 0, hi)),
        out_shape=jax.ShapeDtypeStruct((b, t, DIFF_HEADS * hw), BF16),
        scratch_shapes=[pltpu.VMEM((t, hw), BF16)],
        compiler_params=_cparams(2),
        name="diff_attn",
    )(qkv, qkv, qkv, cos, sin_a, sin_b, lam, subln_g.reshape(1, hw))


def _router_kernel(x_ref, g_ref, sh_ref, sc_ref, rw_ref, rb_ref, h_ref, e_ref, r_ref, gt_ref, cnt_ref, run_ref, *, tpb, ctx_row):
    i = pl.program_id(0)

    @pl.when(i == 0)
    def _():
        run_ref[...] = jnp.zeros_like(run_ref)

    y = _rms(x_ref[...], g_ref[...])
    h = y * (1.0 + _row_mod(i, tpb, ctx_row, sc_ref)) + _row_mod(i, tpb, ctx_row, sh_ref)
    h_ref[...] = h

    h_hi = h.astype(BF16)
    h_lo = (h - h_hi.astype(F32)).astype(BF16)
    w = rw_ref[...]
    w_hi = w.astype(BF16)
    w_lo = (w - w_hi.astype(F32)).astype(BF16)
    logits = (jnp.dot(h_hi, w_hi, preferred_element_type=F32) + jnp.dot(h_lo, w_hi, preferred_element_type=F32)
              + jnp.dot(h_hi, w_lo, preferred_element_type=F32)) + rb_ref[...]

    lane = lax.broadcasted_iota(jnp.int32, (ROW_TILE, LANES), 1)
    lane_f = lane.astype(F32)
    work = jnp.where(lane < N_EXPERTS, logits, -jnp.inf)
    vals, onehots = [], []
    for _ in range(TOP_K):
        m = jnp.max(work, axis=-1, keepdims=True)
        idx = jnp.min(jnp.where(work == m, lane_f, float(LANES)), axis=-1, keepdims=True)
        hit = lane_f == idx
        vals.append(m)
        onehots.append(hit)
        work = jnp.where(hit, -jnp.inf, work)

    exps = [jnp.exp(v - vals[0]) for v in vals]
    denom = exps[0] + exps[1] + exps[2] + exps[3]

    member = onehots[0]
    for k in range(1, TOP_K):
        member = jnp.logical_or(member, onehots[k])
    member_f = member.astype(F32)
    r_i = lax.broadcasted_iota(jnp.int32, (ROW_TILE, ROW_TILE), 0)
    c_i = lax.broadcasted_iota(jnp.int32, (ROW_TILE, ROW_TILE), 1)
    strict_lower = (c_i < r_i).astype(BF16)
    before = jnp.dot(strict_lower, member_f.astype(BF16), preferred_element_type=F32) + run_ref[...]

    e_out = jnp.zeros((ROW_TILE, LANES), F32)
    r_out = jnp.zeros((ROW_TILE, LANES), F32)
    g_out = jnp.zeros((ROW_TILE, LANES), F32)
    for k in range(TOP_K):
        hit = onehots[k]
        e_k = jnp.sum(jnp.where(hit, lane_f, 0.0), axis=-1, keepdims=True)
        r_k = jnp.sum(jnp.where(hit, before, 0.0), axis=-1, keepdims=True)
        e_out = jnp.where(lane == k, e_k, e_out)
        r_out = jnp.where(lane == k, r_k, r_out)
        g_out = jnp.where(lane == k, exps[k] / denom, g_out)
    e_ref[...] = e_out.astype(jnp.int32)
    r_ref[...] = r_out.astype(jnp.int32)
    gt_ref[...] = g_out

    run_ref[...] = run_ref[...] + jnp.sum(member_f, axis=0, keepdims=True)
    cnt_ref[...] = run_ref[...].astype(jnp.int32)


def _router(xs, g, mod, router_w, router_b, tpb, ctx_row):
    n_tok, d = xs.shape
    n_col = D_MODEL // d
    del n_col
    rw = jnp.zeros((d, LANES), F32).at[:, :N_EXPERTS].set(router_w)
    rb = jnp.zeros((1, LANES), F32).at[0, :N_EXPERTS].set(router_b)
    kern = functools.partial(_router_kernel, tpb=tpb, ctx_row=ctx_row)
    row_spec = pl.BlockSpec((ROW_TILE, LANES), lambda i: (i, 0))
    return pl.pallas_call(
        kern,
        grid=(n_tok // ROW_TILE,),
        in_specs=[
            pl.BlockSpec((ROW_TILE, d), lambda i: (i, 0)),
            pl.BlockSpec((1, d), lambda i: (0, 0)),
            pl.BlockSpec((MOD_ROWS, d), lambda i: (0, 3)),
            pl.BlockSpec((MOD_ROWS, d), lambda i: (0, 4)),
            pl.BlockSpec((d, LANES), lambda i: (0, 0)),
            pl.BlockSpec((1, LANES), lambda i: (0, 0)),
        ],
        out_specs=[
            pl.BlockSpec((ROW_TILE, d), lambda i: (i, 0)),
            row_spec, row_spec, row_spec,
            pl.BlockSpec((1, LANES), lambda i: (0, 0)),
        ],
        out_shape=[
            jax.ShapeDtypeStruct((n_tok, d), F32),
            jax.ShapeDtypeStruct((n_tok, LANES), jnp.int32),
            jax.ShapeDtypeStruct((n_tok, LANES), jnp.int32),
            jax.ShapeDtypeStruct((n_tok, LANES), F32),
            jax.ShapeDtypeStruct((1, LANES), jnp.int32),
        ],
        scratch_shapes=[pltpu.VMEM((1, LANES), F32)],
        compiler_params=_cparams(1),
        name="moe_router",
    )(xs, g.reshape(1, d), mod, mod, rw, rb)


DISPATCH_TILE = 256


def _dispatch_kernel(dest_ref, h_ref, init_ref, slots_ref, sem):
    del init_ref

    def row_copy(t, k):
        d = dest_ref[t * TOP_K + k]
        return pltpu.make_async_copy(h_ref.at[pl.ds(t, 1), :], slots_ref.at[pl.ds(d, 1), :], sem)

    def start(t, carry):
        for k in range(TOP_K):
            row_copy(t, k).start()
        return carry

    def wait(t, carry):
        for k in range(TOP_K):
            row_copy(t, k).wait()
        return carry

    lax.fori_loop(0, DISPATCH_TILE, start, 0)
    lax.fori_loop(0, DISPATCH_TILE, wait, 0)


def _dispatch(h, dest_flat, n_slots):
    n_tok, d = h.shape
    init = jnp.zeros((n_slots, d), F32)
    return pl.pallas_call(
        _dispatch_kernel,
        grid=(n_tok // DISPATCH_TILE,),
        in_specs=[
            pl.BlockSpec((DISPATCH_TILE * TOP_K,), lambda i: (i,), memory_space=pltpu.SMEM),
            pl.BlockSpec((DISPATCH_TILE, d), lambda i: (i, 0)),
            pl.BlockSpec(memory_space=pl.ANY),
        ],
        out_specs=pl.BlockSpec(memory_space=pl.ANY),
        out_shape=jax.ShapeDtypeStruct((n_slots, d), F32),
        scratch_shapes=[pltpu.SemaphoreType.DMA(())],
        input_output_aliases={2: 0},
        compiler_params=pltpu.CompilerParams(dimension_semantics=("arbitrary",), vmem_limit_bytes=VMEM_LIMIT, has_side_effects=True),
        name="moe_dispatch",
    )(dest_flat, h, init)


def _expert_kernel(be_ref, nu_ref, x_ref, wg_ref, bg_ref, wu_ref, bu_ref, wd_ref, bd_ref, y_ref):
    b = pl.program_id(0)

    @pl.when(b < nu_ref[0])
    def _():
        x = x_ref[...].astype(BF16)
        g = jnp.minimum(jnp.dot(x, wg_ref[0], preferred_element_type=F32) + bg_ref[0], SWIGLU_LIMIT)
        u = jnp.clip(jnp.dot(x, wu_ref[0], preferred_element_type=F32) + bu_ref[0], -SWIGLU_LIMIT, SWIGLU_LIMIT)
        act = (u + 1.0) * (g * jax.nn.sigmoid(SWIGLU_ALPHA * g))
        y_ref[...] = jnp.dot(act.astype(BF16), wd_ref[0], preferred_element_type=F32) + bd_ref[0]

    @pl.when(b >= nu_ref[0])
    def _():
        y_ref[...] = jnp.zeros_like(y_ref)


def _experts(x_slots, block_expert, n_used, wg, bg, wu, bu, wd, bd):
    n_slots, d = x_slots.shape
    n_blocks = n_slots // EXPERT_TILE
    de = wg.shape[-1]
    grid_spec = pltpu.PrefetchScalarGridSpec(
        num_scalar_prefetch=2,
        grid=(n_blocks,),
        in_specs=[
            pl.BlockSpec((EXPERT_TILE, d), lambda b, be, nu: (b, 0)),
            pl.BlockSpec((1, d, de), lambda b, be, nu: (be[b], 0, 0)),
            pl.BlockSpec((1, 1, de), lambda b, be, nu: (be[b], 0, 0)),
            pl.BlockSpec((1, d, de), lambda b, be, nu: (be[b], 0, 0)),
            pl.BlockSpec((1, 1, de), lambda b, be, nu: (be[b], 0, 0)),
            pl.BlockSpec((1, de, d), lambda b, be, nu: (be[b], 0, 0)),
            pl.BlockSpec((1, 1, d), lambda b, be, nu: (be[b], 0, 0)),
        ],
        out_specs=pl.BlockSpec((EXPERT_TILE, d), lambda b, be, nu: (b, 0)),
    )
    return pl.pallas_call(
        _expert_kernel,
        grid_spec=grid_spec,
        out_shape=jax.ShapeDtypeStruct((n_slots, d), F32),
        compiler_params=_cparams(1),
        name="moe_experts",
    )(block_expert, n_used, x_slots, wg, bg.reshape(N_EXPERTS, 1, de), wu, bu.reshape(N_EXPERTS, 1, de), wd, bd.reshape(N_EXPERTS, 1, d))


COMBINE_TILE = 128


def _combine_kernel(dest_ref, y_hbm, gt_ref, x_ref, gate_ref, o_ref, buf, sem, *, tiles_per_batch, ctx_tiles, ctx_row):
    i = pl.program_id(0)

    def row_copy(t, k):
        d = dest_ref[t * TOP_K + k]
        return pltpu.make_async_copy(y_hbm.at[pl.ds(d, 1), :], buf.at[k, pl.ds(t, 1), :], sem)

    def start(t, carry):
        for k in range(TOP_K):
            row_copy(t, k).start()
        return carry

    def wait(t, carry):
        for k in range(TOP_K):
            row_copy(t, k).wait()
        return carry

    lax.fori_loop(0, COMBINE_TILE, start, 0)
    lax.fori_loop(0, COMBINE_TILE, wait, 0)

    gates = gt_ref[...]
    y = gates[:, 0:1] * buf[0]
    for k in range(1, TOP_K):
        y = y + gates[:, k:k + 1] * buf[k]
    b = i // tiles_per_batch
    is_ctx = (i % tiles_per_batch) < ctx_tiles
    mod_gate = jnp.where(is_ctx, gate_ref[ctx_row:ctx_row + 1, :], gate_ref[pl.ds(b, 1), :])
    o_ref[...] = x_ref[...] + mod_gate * y


def _combine(y_slots, dest_flat, gates, xs, mod, rows_per_batch, ctx_row):
    n_tok, d = xs.shape
    kern = functools.partial(_combine_kernel, tiles_per_batch=rows_per_batch // COMBINE_TILE,
                             ctx_tiles=CTX_LEN // COMBINE_TILE, ctx_row=ctx_row)
    return pl.pallas_call(
        kern,
        grid=(n_tok // COMBINE_TILE,),
        in_specs=[
            pl.BlockSpec((COMBINE_TILE * TOP_K,), lambda i: (i,), memory_space=pltpu.SMEM),
            pl.BlockSpec(memory_space=pl.ANY),
            pl.BlockSpec((COMBINE_TILE, LANES), lambda i: (i, 0)),
            pl.BlockSpec((COMBINE_TILE, d), lambda i: (i, 0)),
            pl.BlockSpec((MOD_ROWS, d), lambda i: (0, 5)),
        ],
        out_specs=pl.BlockSpec((COMBINE_TILE, d), lambda i: (i, 0)),
        out_shape=jax.ShapeDtypeStruct((n_tok, d), F32),
        scratch_shapes=[pltpu.VMEM((TOP_K, COMBINE_TILE, d), F32), pltpu.SemaphoreType.DMA(())],
        compiler_params=_cparams(1),
        name="moe_combine",
    )(dest_flat, y_slots, gates, xs, mod)


def _final_norm_kernel(x_ref, g_ref, o_ref):
    o_ref[0] = _rms(x_ref[0], g_ref[...])


def _final_norm(xs3, g):
    b, t, d = xs3.shape
    blk = CTX_LEN
    n_lat = t - CTX_LEN
    return pl.pallas_call(
        _final_norm_kernel,
        grid=(b, n_lat // blk),
        in_specs=[
            pl.BlockSpec((1, blk, d), lambda bi, j: (bi, j + 1, 0)),
            pl.BlockSpec((1, d), lambda bi, j: (0, 0)),
        ],
        out_specs=pl.BlockSpec((1, blk, d), lambda bi, j: (bi, j, 0)),
        out_shape=jax.ShapeDtypeStruct((b, n_lat, d), F32),
        compiler_params=_cparams(2),
        name="final_norm",
    )(xs3, g.reshape(1, d))


def _moe_layer(xs, g, mod, rows_per_batch, ctx_row, router_w, router_b, wg, bg, wu, bu, wd, bd):
    n_tok, d = xs.shape
    tpb = rows_per_batch // ROW_TILE
    h, e_pad, r_pad, gates, cnt = _router(xs, g, mod, router_w, router_b, tpb, ctx_row)
    counts = cnt[0, :N_EXPERTS]
    padded = ((counts + EXPERT_TILE - 1) // EXPERT_TILE) * EXPERT_TILE
    pend = jnp.cumsum(padded)
    pstart = pend - padded
    n_blocks = (n_tok * TOP_K) // EXPERT_TILE + N_EXPERTS
    dest = (pstart[e_pad[:, :TOP_K]] + r_pad[:, :TOP_K]).reshape(-1).astype(jnp.int32)
    block_expert = jnp.clip(jnp.searchsorted(pend, jnp.arange(n_blocks, dtype=jnp.int32) * EXPERT_TILE, side='right'),
                            0, N_EXPERTS - 1).astype(jnp.int32)
    n_used = (pend[-1:] // EXPERT_TILE).astype(jnp.int32)
    x_slots = _dispatch(h, dest, n_blocks * EXPERT_TILE)
    y_slots = _experts(x_slots, block_expert, n_used, wg, bg, wu, bu, wd, bd)
    return _combine(y_slots, dest, gates, xs, mod, rows_per_batch, ctx_row)


def kernel(x, c, ctx, c_ctx, ada_w, ada_b, norm_mix_g, norm_ffn_g, final_g, na_wqkv, na_wo, na_rpb, gqa_wqkv, gqa_wo, gqa_q_g, gqa_k_g, diff_wqkv, diff_wo, diff_lambda, diff_subln_g, router_w, router_b, w_gate, b_gate, w_up, b_up, w_down, b_down):
    B, S, D = x.shape
    L = ctx.shape[1]
    T = L + S
    assert D == D_MODEL and L == CTX_LEN and T % ROW_TILE == 0 and B < MOD_ROWS
    tpb = T // ROW_TILE
    ctx_row = B

    cvec = jnp.zeros((MOD_ROWS, D), F32).at[:B].set(c).at[B].set(c_ctx)
    mods = _ada_mod(cvec, ada_w, ada_b)
    rope = _rope_tables(S)

    xs = jnp.concatenate([ctx, x], axis=1).reshape(B * T, D)
    for i in range(DEPTH):
        need_ctx = i < DEPTH - 1
        mod = mods[i]
        kind, j = i % 3, i // 3
        if kind == 0:
            qkv = _norm_mm(xs, norm_mix_g[i], mod, na_wqkv[j].astype(BF16), tpb, ctx_row).reshape(B, T, -1)
            o = _na_attention(qkv, _na_bias_table(na_rpb[j]), need_ctx)
            w_o = na_wo[j]
        elif kind == 1:
            qkv = _norm_mm(xs, norm_mix_g[i], mod, gqa_wqkv[j].astype(BF16), tpb, ctx_row).reshape(B, T, -1)
            o = _gqa_attention(qkv, rope, gqa_q_g[j], gqa_k_g[j], need_ctx)
            w_o = gqa_wo[j]
        else:
            qkv = _norm_mm(xs, norm_mix_g[i], mod, diff_wqkv[j].astype(BF16), tpb, ctx_row).reshape(B, T, -1)
            o = _diff_attention(qkv, rope, diff_lambda[j], diff_subln_g[j], i, need_ctx)
            w_o = diff_wo[j]
        xs = _out_proj(o.reshape(B * T, D), w_o.astype(BF16), xs, mod, tpb, ctx_row)
        xs = _moe_layer(xs, norm_ffn_g[i], mod, T, ctx_row, router_w[i], router_b[i],
                        w_gate[i].astype(BF16), b_gate[i], w_up[i].astype(BF16), b_up[i], w_down[i].astype(BF16), b_down[i])
    return _final_norm(xs.reshape(B, T, D), final_g)
```

```python
import functools
import math

import jax
import jax.numpy as jnp
from jax import lax
from jax.experimental import pallas as pl
from jax.experimental.pallas import tpu as pltpu

F32 = jnp.float32
BF16 = jnp.bfloat16

D_MODEL = 2048
DEPTH = 4
GRID_W = 64
CTX_LEN = 256
HEAD_DIM = 128
ROPE_THETA = 10000.0
NORM_EPS = 1e-6
MASK_VALUE = -1e30
NA_HEADS = 16
NA_WIN_H = 8
NA_WIN_W = 16
GQA_HEADS = 16
GQA_KV_HEADS = 4
DIFF_HEADS = 8
N_EXPERTS = 32
TOP_K = 4
D_EXPERT = 768
SWIGLU_LIMIT = 7.0
SWIGLU_ALPHA = 1.702

LANES = 128
ROW_TILE = 768
EXPERT_TILE = 256
MOD_ROWS = 8
VMEM_LIMIT = 56 * 1024 * 1024

_NT = (((1,), (1,)), ((), ()))
LOG2_E = math.log2(math.e)
SCORE_SCALE = HEAD_DIM ** -0.5 * LOG2_E


def _cparams(n_axes, vmem=VMEM_LIMIT):
    return pltpu.CompilerParams(dimension_semantics=("arbitrary",) * n_axes, vmem_limit_bytes=vmem)


def _row_mod(tile, tiles_per_batch, ctx_row, ref):
    b = tile // tiles_per_batch
    rows = lax.broadcasted_iota(jnp.int32, (ROW_TILE, 1), 0)
    is_ctx = jnp.logical_and(tile % tiles_per_batch == 0, rows < CTX_LEN)
    return jnp.where(is_ctx, ref[ctx_row:ctx_row + 1, :], ref[pl.ds(b, 1), :])


def _rms(x, g):
    return x * lax.rsqrt(jnp.mean(x * x, axis=-1, keepdims=True) + NORM_EPS) * g


def _softmax_parts(s):
    m = jnp.max(s, axis=-1, keepdims=True)
    p = jnp.exp2(s - m)
    return p, jnp.sum(p, axis=-1, keepdims=True)


def _ada_kernel(c_ref, w_ref, b_ref, o_ref):
    c = c_ref[...]
    s = c * jax.nn.sigmoid(c)
    o_ref[0] = jnp.dot(s.astype(BF16), w_ref[0].astype(BF16), preferred_element_type=F32) + b_ref[0]


def _ada_mod(cvec, ada_w, ada_b):
    depth, d, n = ada_w.shape
    tn = 1024
    return pl.pallas_call(
        _ada_kernel,
        grid=(depth, n // tn),
        in_specs=[
            pl.BlockSpec((MOD_ROWS, d), lambda l, j: (0, 0)),
            pl.BlockSpec((1, d, tn), lambda l, j: (l, 0, j)),
            pl.BlockSpec((1, 1, tn), lambda l, j: (l, 0, j)),
        ],
        out_specs=pl.BlockSpec((1, MOD_ROWS, tn), lambda l, j: (l, 0, j)),
        out_shape=jax.ShapeDtypeStruct((depth, MOD_ROWS, n), F32),
        compiler_params=_cparams(2),
        name="ada_mod",
    )(cvec, ada_w, ada_b.reshape(depth, 1, n))


def _norm_mm_kernel(x_ref, g_ref, sh_ref, sc_ref, w_ref, o_ref, h_ref, *, tpb, ctx_row):
    i = pl.program_id(0)

    @pl.when(pl.program_id(1) == 0)
    def _():
        y = _rms(x_ref[...], g_ref[...])
        sh = _row_mod(i, tpb, ctx_row, sh_ref)
        sc = _row_mod(i, tpb, ctx_row, sc_ref)
        h_ref[...] = (y * (1.0 + sc) + sh).astype(BF16)

    o_ref[...] = jnp.dot(h_ref[...], w_ref[...], preferred_element_type=F32).astype(BF16)


def _norm_mm(xs, g, mod, w_bf16, tpb, ctx_row):
    n_tok, d = xs.shape
    n_out = w_bf16.shape[1]
    tn = 1024
    kern = functools.partial(_norm_mm_kernel, tpb=tpb, ctx_row=ctx_row)
    return pl.pallas_call(
        kern,
        grid=(n_tok // ROW_TILE, n_out // tn),
        in_specs=[
            pl.BlockSpec((ROW_TILE, d), lambda i, j: (i, 0)),
            pl.BlockSpec((1, d), lambda i, j: (0, 0)),
            pl.BlockSpec((MOD_ROWS, d), lambda i, j: (0, 0)),
            pl.BlockSpec((MOD_ROWS, d), lambda i, j: (0, 1)),
            pl.BlockSpec((d, tn), lambda i, j: (0, j)),
        ],
        out_specs=pl.BlockSpec((ROW_TILE, tn), lambda i, j: (i, j)),
        out_shape=jax.ShapeDtypeStruct((n_tok, n_out), BF16),
        scratch_shapes=[pltpu.VMEM((ROW_TILE, d), BF16)],
        compiler_params=_cparams(2),
        name="norm_qkv",
    )(xs, g.reshape(1, d), mod, mod, w_bf16)


def _out_proj_kernel(o_ref, w_ref, x_ref, gate_ref, y_ref, *, tpb, ctx_row):
    i = pl.program_id(0)
    gate = _row_mod(i, tpb, ctx_row, gate_ref)
    y = jnp.dot(o_ref[...], w_ref[...], preferred_element_type=F32)
    y_ref[...] = x_ref[...] + gate * y


def _out_proj(o, w_bf16, xs, mod, tpb, ctx_row):
    n_tok, d = xs.shape
    tn = 1024
    kern = functools.partial(_out_proj_kernel, tpb=tpb, ctx_row=ctx_row)
    return pl.pallas_call(
        kern,
        grid=(n_tok // ROW_TILE, d // tn),
        in_specs=[
            pl.BlockSpec((ROW_TILE, d), lambda i, j: (i, 0)),
            pl.BlockSpec((d, tn), lambda i, j: (0, j)),
            pl.BlockSpec((ROW_TILE, tn), lambda i, j: (i, j)),
            pl.BlockSpec((MOD_ROWS, tn), lambda i, j: (0, 2 * (D_MODEL // tn) + j)),
        ],
        out_specs=pl.BlockSpec((ROW_TILE, tn), lambda i, j: (i, j)),
        out_shape=jax.ShapeDtypeStruct((n_tok, d), F32),
        compiler_params=_cparams(2),
        name="out_proj",
    )(o, w_bf16, xs, mod)


def _na_kernel(q_ref, k_ref, v_ref, tab_ref, o_ref, *, need_ctx, n_lat):
    scale = SCORE_SCALE
    L = CTX_LEN
    kc = k_ref[0, :L, :]
    vc = v_ref[0, :L, :]
    if need_ctx:
        s = lax.dot_general(q_ref[0, :L, :], kc, _NT, preferred_element_type=F32) * scale
        p, l = _softmax_parts(s)
        o = jnp.dot(p.astype(BF16), vc, preferred_element_type=F32) / l
        o_ref[0, :L, :] = o.astype(BF16)
    else:
        o_ref[0, :L, :] = jnp.zeros((L, HEAD_DIM), BF16)

    rows = n_lat // GRID_W
    qb_rows = 2
    win_rows = 10
    n_blocks = rows // qb_rows

    def body(mq, carry):
        q0 = pl.multiple_of(L + mq * (qb_rows * GRID_W), LANES)
        q = q_ref[0, pl.ds(q0, qb_rows * GRID_W), :]
        u0 = jnp.clip(qb_rows * mq - NA_WIN_H // 2, 0, rows - win_rows)
        k0 = pl.multiple_of(L + u0 * GRID_W, LANES)
        kw = k_ref[0, pl.ds(k0, win_rows * GRID_W), :]
        vw = v_ref[0, pl.ds(k0, win_rows * GRID_W), :]
        s_c = lax.dot_general(q, kc, _NT, preferred_element_type=F32) * scale
        s_l = lax.dot_general(q, kw, _NT, preferred_element_type=F32) * scale
        bias_rows = []
        for a in range(qb_rows):
            qr = qb_rows * mq + a
            r0q = jnp.clip(qr - NA_WIN_H // 2, 0, rows - NA_WIN_H)
            tiles = []
            for p in range(win_rows // 2):
                kr0 = u0 + 2 * p
                kr1 = kr0 + 1
                i0 = jnp.where(jnp.logical_and(kr0 >= r0q, kr0 < r0q + NA_WIN_H), kr0 - qr + NA_WIN_H - 1, 2 * NA_WIN_H - 1)
                i1 = jnp.where(jnp.logical_and(kr1 >= r0q, kr1 < r0q + NA_WIN_H), kr1 - qr + NA_WIN_H - 1, 2 * NA_WIN_H - 1)
                tiles.append(tab_ref[0, 0, i0] + tab_ref[0, 1, i1])
            bias_rows.append(jnp.concatenate(tiles, axis=1))
        s_l = s_l + jnp.concatenate(bias_rows, axis=0)
        m = jnp.maximum(jnp.max(s_c, axis=-1, keepdims=True), jnp.max(s_l, axis=-1, keepdims=True))
        p_c = jnp.exp2(s_c - m)
        p_l = jnp.exp2(s_l - m)
        l = jnp.sum(p_c, axis=-1, keepdims=True) + jnp.sum(p_l, axis=-1, keepdims=True)
        o = jnp.dot(p_c.astype(BF16), vc, preferred_element_type=F32) + jnp.dot(p_l.astype(BF16), vw, preferred_element_type=F32)
        o_ref[0, pl.ds(q0, qb_rows * GRID_W), :] = (o / l).astype(BF16)
        return carry

    lax.fori_loop(0, n_blocks, body, 0, unroll=2)


def _na_bias_table(rpb):
    col = jnp.arange(GRID_W)
    col_start = jnp.clip(col - NA_WIN_W // 2, 0, GRID_W - NA_WIN_W)
    col_mask = (col[None, :] >= col_start[:, None]) & (col[None, :] < col_start[:, None] + NA_WIN_W)
    dc_idx = jnp.clip(col[None, :] - col[:, None] + NA_WIN_W - 1, 0, 2 * NA_WIN_W - 2)
    tab = rpb[:, :, dc_idx] * LOG2_E
    tab = jnp.where(col_mask[None, None], tab, MASK_VALUE)
    tab = jnp.concatenate([tab, jnp.full_like(tab[:, :1], MASK_VALUE)], axis=1)
    zeros = jnp.zeros_like(tab)
    left = jnp.concatenate([tab, zeros], axis=-1)
    right = jnp.concatenate([zeros, tab], axis=-1)
    return jnp.stack([left, right], axis=1).astype(F32)


def _na_attention(qkv, tab, need_ctx):
    b, t, _ = qkv.shape
    h = NA_HEADS
    kern = functools.partial(_na_kernel, need_ctx=need_ctx, n_lat=t - CTX_LEN)
    return pl.pallas_call(
        kern,
        grid=(b, h),
        in_specs=[
            pl.BlockSpec((1, t, HEAD_DIM), lambda bi, hi: (bi, 0, hi)),
            pl.BlockSpec((1, t, HEAD_DIM), lambda bi, hi: (bi, 0, h + hi)),
            pl.BlockSpec((1, t, HEAD_DIM), lambda bi, hi: (bi, 0, 2 * h + hi)),
            pl.BlockSpec((1, 2, 2 * NA_WIN_H, GRID_W, LANES), lambda bi, hi: (hi, 0, 0, 0, 0)),
        ],
        out_specs=pl.BlockSpec((1, t, HEAD_DIM), lambda bi, hi: (bi, 0, hi)),
        out_shape=jax.ShapeDtypeStruct((b, t, h * HEAD_DIM), BF16),
        compiler_params=_cparams(2),
        name="na_attn",
    )(qkv, qkv, qkv, tab)


def _rope_tables(n_tok):
    t = jnp.arange(n_tok)
    row = (t // GRID_W).astype(F32)
    col = (t % GRID_W).astype(F32)
    half = HEAD_DIM // 2
    inv_freq = ROPE_THETA ** (-jnp.arange(0, half, 2, dtype=F32) / half)
    ang_r = row[:, None] * inv_freq[None, :]
    ang_c = col[:, None] * inv_freq[None, :]
    ang = jnp.concatenate([ang_r, ang_r, ang_c, ang_c], axis=-1)
    cos, sin = jnp.cos(ang), jnp.sin(ang)
    upper = (jnp.arange(HEAD_DIM) % half) >= (half // 2)
    sin_a = jnp.where(upper[None, :], sin, 0.0)
    sin_b = jnp.where(upper[None, :], 0.0, -sin)
    return cos, sin_a, sin_b


def _rope(x, cos, sin_a, sin_b):
    quarter = HEAD_DIM // 4
    return x * cos + pltpu.roll(x, quarter, 1) * sin_a + pltpu.roll(x, HEAD_DIM - quarter, 1) * sin_b


def _gqa_kernel(q_ref, k_ref, v_ref, cos_ref, sa_ref, sb_ref, qg_ref, kg_ref, o_ref, kn_ref, *, need_ctx, n_lat):
    L = CTX_LEN
    G = GQA_HEADS // GQA_KV_HEADS
    scale = SCORE_SCALE
    tq = 128

    kg = kg_ref[...]
    kn_ref[:L, :] = _rms(k_ref[0, :L, :].astype(F32), kg).astype(BF16)
    k_lat = _rms(k_ref[0, L:, :].astype(F32), kg)
    kn_ref[L:, :] = _rope(k_lat, cos_ref[...], sa_ref[...], sb_ref[...]).astype(BF16)

    qg = qg_ref[...] * scale

    if need_ctx:
        qs = [_rms(q_ref[0, :L, hh * HEAD_DIM:(hh + 1) * HEAD_DIM].astype(F32), qg).astype(BF16) for hh in range(G)]
        s = lax.dot_general(jnp.concatenate(qs, axis=0), kn_ref[:L, :], _NT, preferred_element_type=F32)
        p, l = _softmax_parts(s)
        o = jnp.dot(p.astype(BF16), v_ref[0, :L, :], preferred_element_type=F32) / l
        for hh in range(G):
            o_ref[0, :L, hh * HEAD_DIM:(hh + 1) * HEAD_DIM] = o[hh * L:(hh + 1) * L].astype(BF16)
    else:
        o_ref[0, :L, :] = jnp.zeros((L, G * HEAD_DIM), BF16)

    def body(mq, carry):
        r0 = pl.multiple_of(mq * tq, tq)
        q0 = pl.multiple_of(L + mq * tq, tq)
        cos = cos_ref[pl.ds(r0, tq), :]
        sa = sa_ref[pl.ds(r0, tq), :]
        sb = sb_ref[pl.ds(r0, tq), :]
        for hh in range(G):
            qn = _rms(q_ref[0, pl.ds(q0, tq), hh * HEAD_DIM:(hh + 1) * HEAD_DIM].astype(F32), qg)
            qh = _rope(qn, cos, sa, sb).astype(BF16)
            s = lax.dot_general(qh, kn_ref[...], _NT, preferred_element_type=F32)
            p, l = _softmax_parts(s)
            o = jnp.dot(p.astype(BF16), v_ref[0], preferred_element_type=F32) / l
            o_ref[0, pl.ds(q0, tq), hh * HEAD_DIM:(hh + 1) * HEAD_DIM] = o.astype(BF16)
        return carry

    lax.fori_loop(0, n_lat // tq, body, 0, unroll=2)


def _gqa_attention(qkv, rope, q_g, k_g, need_ctx):
    b, t, _ = qkv.shape
    n_lat = t - CTX_LEN
    G = GQA_HEADS // GQA_KV_HEADS
    gw = G * HEAD_DIM
    cos, sin_a, sin_b = rope
    kern = functools.partial(_gqa_kernel, need_ctx=need_ctx, n_lat=n_lat)
    tab_spec = pl.BlockSpec((n_lat, HEAD_DIM), lambda bi, gi: (0, 0))
    vec_spec = pl.BlockSpec((1, HEAD_DIM), lambda bi, gi: (0, 0))
    return pl.pallas_call(
        kern,
        grid=(b, GQA_KV_HEADS),
        in_specs=[
            pl.BlockSpec((1, t, gw), lambda bi, gi: (bi, 0, gi)),
            pl.BlockSpec((1, t, HEAD_DIM), lambda bi, gi: (bi, 0, GQA_HEADS + gi)),
            pl.BlockSpec((1, t, HEAD_DIM), lambda bi, gi: (bi, 0, GQA_HEADS + GQA_KV_HEADS + gi)),
            tab_spec, tab_spec, tab_spec, vec_spec, vec_spec,
        ],
        out_specs=pl.BlockSpec((1, t, gw), lambda bi, gi: (bi, 0, gi)),
        out_shape=jax.ShapeDtypeStruct((b, t, GQA_HEADS * HEAD_DIM), BF16),
        scratch_shapes=[pltpu.VMEM((t, HEAD_DIM), BF16)],
        compiler_params=_cparams(2),
        name="gqa_attn",
    )(qkv, qkv, qkv, cos, sin_a, sin_b, q_g.reshape(1, HEAD_DIM), k_g.reshape(1, HEAD_DIM))


def _diff_kernel(q_ref, k_ref, v_ref, cos_ref, sa_ref, sb_ref, lam_ref, g_ref, o_ref, kr_ref, *, need_ctx, n_lat, lambda_init):
    L = CTX_LEN
    scale = SCORE_SCALE
    tq = 128
    lam = lam_ref[...]
    lam_full = (jnp.exp(jnp.sum(lam[0:1] * lam[1:2], axis=-1, keepdims=True))
                - jnp.exp(jnp.sum(lam[2:3] * lam[3:4], axis=-1, keepdims=True)) + lambda_init)
    g = g_ref[...] * (1.0 - lambda_init)

    kr_ref[:L, :] = k_ref[0, :L, :]
    for c in range(2):
        k_lat = k_ref[0, L:, c * HEAD_DIM:(c + 1) * HEAD_DIM].astype(F32)
        kr_ref[L:, c * HEAD_DIM:(c + 1) * HEAD_DIM] = _rope(k_lat, cos_ref[...], sa_ref[...], sb_ref[...]).astype(BF16)

    def attend(qs, n_keys):
        ps = []
        for c in range(2):
            s = lax.dot_general(qs[c], kr_ref[:n_keys, c * HEAD_DIM:(c + 1) * HEAD_DIM], _NT, preferred_element_type=F32)
            p, l = _softmax_parts(s)
            ps.append(p * ((1.0 if c == 0 else lam_full) / l))
        a = (ps[0] - ps[1]).astype(BF16)
        o = jnp.dot(a, v_ref[0, :n_keys, :], preferred_element_type=F32)
        return _rms(o, g).astype(BF16)

    if need_ctx:
        qs = [(q_ref[0, :L, c * HEAD_DIM:(c + 1) * HEAD_DIM].astype(F32) * scale).astype(BF16) for c in range(2)]
        o_ref[0, :L, :] = attend(qs, L)
    else:
        o_ref[0, :L, :] = jnp.zeros((L, 2 * HEAD_DIM), BF16)

    def body(mq, carry):
        r0 = pl.multiple_of(mq * tq, tq)
        q0 = pl.multiple_of(L + mq * tq, tq)
        cos = cos_ref[pl.ds(r0, tq), :]
        sa = sa_ref[pl.ds(r0, tq), :]
        sb = sb_ref[pl.ds(r0, tq), :]
        qs = []
        for c in range(2):
            q = q_ref[0, pl.ds(q0, tq), c * HEAD_DIM:(c + 1) * HEAD_DIM].astype(F32) * scale
            qs.append(_rope(q, cos, sa, sb).astype(BF16))
        o_ref[0, pl.ds(q0, tq), :] = attend(qs, L + n_lat)
        return carry

    lax.fori_loop(0, n_lat // tq, body, 0, unroll=2)


def _diff_attention(qkv, rope, lam, subln_g, layer_idx, need_ctx):
    b, t, _ = qkv.shape
    n_lat = t - CTX_LEN
    hw = 2 * HEAD_DIM
    cos, sin_a, sin_b = rope
    lambda_init = 0.8 - 0.6 * math.exp(-0.3 * layer_idx)
    kern = functools.partial(_diff_kernel, need_ctx=need_ctx, n_lat=n_lat, lambda_init=lambda_init)
    tab_spec = pl.BlockSpec((n_lat, HEAD_DIM), lambda bi, hi: (0, 0))
    return pl.pallas_call(
        kern,
        grid=(b, DIFF_HEADS),
        in_specs=[
            pl.BlockSpec((1, t, hw), lambda bi, hi: (bi, 0, hi)),
            pl.BlockSpec((1, t, hw), lambda bi, hi: (bi, 0, DIFF_HEADS + hi)),
            pl.BlockSpec((1, t, hw), lambda bi, hi: (bi, 0, 2 * DIFF_HEADS + hi)),
            tab_spec, tab_spec, tab_spec,
            pl.BlockSpec((4, HEAD_DIM), lambda bi, hi: (0, 0)),
            pl.BlockSpec((1, hw), lambda bi, hi: (0, 0)),
        ],
        out_specs=pl.BlockSpec((1, t, hw), lambda bi, hi: (bi, 0, hi)),
        out_shape=jax.ShapeDtypeStruct((b, t, DIFF_HEADS * hw), BF16),
        scratch_shapes=[pltpu.VMEM((t, hw), BF16)],
        compiler_params=_cparams(2),
        name="diff_attn",
    )(qkv, qkv, qkv, cos, sin_a, sin_b, lam, subln_g.reshape(1, hw))


def _router_kernel(x_ref, g_ref, sh_ref, sc_ref, rw_ref, rb_ref, h_ref, e_ref, r_ref, gt_ref, cnt_ref, run_ref, *, tpb, ctx_row):
    i = pl.program_id(0)

    @pl.when(i == 0)
    def _():
        run_ref[...] = jnp.zeros_like(run_ref)

    y = _rms(x_ref[...], g_ref[...])
    h = y * (1.0 + _row_mod(i, tpb, ctx_row, sc_ref)) + _row_mod(i, tpb, ctx_row, sh_ref)
    h_ref[...] = h

    h_hi = h.astype(BF16)
    h_lo = (h - h_hi.astype(F32)).astype(BF16)
    w = rw_ref[...]
    w_hi = w.astype(BF16)
    w_lo = (w - w_hi.astype(F32)).astype(BF16)
    logits = (jnp.dot(h_hi, w_hi, preferred_element_type=F32) + jnp.dot(h_lo, w_hi, preferred_element_type=F32)
              + jnp.dot(h_hi, w_lo, preferred_element_type=F32)) + rb_ref[...]

    lane = lax.broadcasted_iota(jnp.int32, (ROW_TILE, LANES), 1)
    lane_f = lane.astype(F32)
    work = jnp.where(lane < N_EXPERTS, logits, -jnp.inf)
    vals, onehots = [], []
    for _ in range(TOP_K):
        m = jnp.max(work, axis=-1, keepdims=True)
        idx = jnp.min(jnp.where(work == m, lane_f, float(LANES)), axis=-1, keepdims=True)
        hit = lane_f == idx
        vals.append(m)
        onehots.append(hit)
        work = jnp.where(hit, -jnp.inf, work)

    exps = [jnp.exp(v - vals[0]) for v in vals]
    denom = exps[0] + exps[1] + exps[2] + exps[3]

    member = onehots[0]
    for k in range(1, TOP_K):
        member = jnp.logical_or(member, onehots[k])
    member_f = member.astype(F32)
    r_i = lax.broadcasted_iota(jnp.int32, (ROW_TILE, ROW_TILE), 0)
    c_i = lax.broadcasted_iota(jnp.int32, (ROW_TILE, ROW_TILE), 1)
    strict_lower = (c_i < r_i).astype(BF16)
    before = jnp.dot(strict_lower, member_f.astype(BF16), preferred_element_type=F32) + run_ref[...]

    e_out = jnp.zeros((ROW_TILE, LANES), F32)
    r_out = jnp.zeros((ROW_TILE, LANES), F32)
    g_out = jnp.zeros((ROW_TILE, LANES), F32)
    for k in range(TOP_K):
        hit = onehots[k]
        e_k = jnp.sum(jnp.where(hit, lane_f, 0.0), axis=-1, keepdims=True)
        r_k = jnp.sum(jnp.where(hit, before, 0.0), axis=-1, keepdims=True)
        e_out = jnp.where(lane == k, e_k, e_out)
        r_out = jnp.where(lane == k, r_k, r_out)
        g_out = jnp.where(lane == k, exps[k] / denom, g_out)
    e_ref[...] = e_out.astype(jnp.int32)
    r_ref[...] = r_out.astype(jnp.int32)
    gt_ref[...] = g_out

    run_ref[...] = run_ref[...] + jnp.sum(member_f, axis=0, keepdims=True)
    cnt_ref[...] = run_ref[...].astype(jnp.int32)


def _router(xs, g, mod, router_w, router_b, tpb, ctx_row):
    n_tok, d = xs.shape
    n_col = D_MODEL // d
    del n_col
    rw = jnp.zeros((d, LANES), F32).at[:, :N_EXPERTS].set(router_w)
    rb = jnp.zeros((1, LANES), F32).at[0, :N_EXPERTS].set(router_b)
    kern = functools.partial(_router_kernel, tpb=tpb, ctx_row=ctx_row)
    row_spec = pl.BlockSpec((ROW_TILE, LANES), lambda i: (i, 0))
    return pl.pallas_call(
        kern,
        grid=(n_tok // ROW_TILE,),
        in_specs=[
            pl.BlockSpec((ROW_TILE, d), lambda i: (i, 0)),
            pl.BlockSpec((1, d), lambda i: (0, 0)),
            pl.BlockSpec((MOD_ROWS, d), lambda i: (0, 3)),
            pl.BlockSpec((MOD_ROWS, d), lambda i: (0, 4)),
            pl.BlockSpec((d, LANES), lambda i: (0, 0)),
            pl.BlockSpec((1, LANES), lambda i: (0, 0)),
        ],
        out_specs=[
            pl.BlockSpec((ROW_TILE, d), lambda i: (i, 0)),
            row_spec, row_spec, row_spec,
            pl.BlockSpec((1, LANES), lambda i: (0, 0)),
        ],
        out_shape=[
            jax.ShapeDtypeStruct((n_tok, d), F32),
            jax.ShapeDtypeStruct((n_tok, LANES), jnp.int32),
            jax.ShapeDtypeStruct((n_tok, LANES), jnp.int32),
            jax.ShapeDtypeStruct((n_tok, LANES), F32),
            jax.ShapeDtypeStruct((1, LANES), jnp.int32),
        ],
        scratch_shapes=[pltpu.VMEM((1, LANES), F32)],
        compiler_params=_cparams(1),
        name="moe_router",
    )(xs, g.reshape(1, d), mod, mod, rw, rb)


DISPATCH_TILE = 256


def _dispatch_kernel(pad_start_ref, pad_len_ref, n_used_ref, dest_ref, h_ref, slots_ref, zero_ref, sem, pad_sem, tail_sem):
    def row_copy(t, k):
        d = dest_ref[t * TOP_K + k]
        return pltpu.make_async_copy(h_ref.at[pl.ds(t, 1), :], slots_ref.at[pl.ds(d, 1), :], sem)

    def start(t, carry):
        for k in range(TOP_K):
            row_copy(t, k).start(priority=k % 2)
        return carry

    def wait(t, carry):
        for k in range(TOP_K):
            row_copy(t, k).wait()
        return carry

    lax.fori_loop(0, DISPATCH_TILE, start, 0)

    @pl.when(pl.program_id(0) == 0)
    def _():
        zero_ref[...] = jnp.zeros_like(zero_ref)

        def pad_copy(e, r):
            return pltpu.make_async_copy(zero_ref.at[pl.ds(0, 1), :], slots_ref.at[pl.ds(pad_start_ref[e] + r, 1), :], pad_sem)

        def for_each_pad(fn):
            def per_expert(e, carry):
                lax.fori_loop(0, pad_len_ref[e], lambda r, c: fn(e, r) or c, 0)
                return carry
            lax.fori_loop(0, N_EXPERTS, per_expert, 0)

        def tail_copy(b):
            rows = pl.ds(pl.multiple_of(b * EXPERT_TILE, EXPERT_TILE), EXPERT_TILE)
            return pltpu.make_async_copy(zero_ref, slots_ref.at[rows, :], tail_sem)

        def for_each_tail(fn):
            lax.fori_loop(n_used_ref[0], slots_ref.shape[0] // EXPERT_TILE, lambda b, c: fn(b) or c, 0)

        for_each_pad(lambda e, r: pad_copy(e, r).start())
        for_each_tail(lambda b: tail_copy(b).start())
        for_each_pad(lambda e, r: pad_copy(e, r).wait())
        for_each_tail(lambda b: tail_copy(b).wait())

    lax.fori_loop(0, DISPATCH_TILE, wait, 0)


def _dispatch(h, dest_flat, pad_start, pad_len, n_used, n_slots):
    n_tok, d = h.shape
    grid_spec = pltpu.PrefetchScalarGridSpec(
        num_scalar_prefetch=3,
        grid=(n_tok // DISPATCH_TILE,),
        in_specs=[
            pl.BlockSpec((DISPATCH_TILE * TOP_K,), lambda i, ps, pn, nu: (i,), memory_space=pltpu.SMEM),
            pl.BlockSpec((DISPATCH_TILE, d), lambda i, ps, pn, nu: (i, 0)),
        ],
        out_specs=pl.BlockSpec(memory_space=pl.ANY),
        scratch_shapes=[pltpu.VMEM((EXPERT_TILE, d), F32)] + [pltpu.SemaphoreType.DMA(())] * 3,
    )
    return pl.pallas_call(
        _dispatch_kernel,
        grid_spec=grid_spec,
        out_shape=jax.ShapeDtypeStruct((n_slots, d), F32),
        compiler_params=_cparams(1),
        name="moe_dispatch",
    )(pad_start, pad_len, n_used, dest_flat, h)


CAST_ROWS = 256


def _expert_kernel(be_ref, nx_ref, nu_ref, x_ref, bg_ref, bu_ref, bd_ref, wg_hbm, wu_hbm, wd_hbm, y_ref,
                   stage_g, stage_u, stage_d, wg_bf, wu_bf, wd_bf, sem, *, layer):
    b = pl.program_id(0)
    used = b < nu_ref[0]
    e = be_ref[b]
    first = jnp.logical_or(b == 0, be_ref[jnp.maximum(b - 1, 0)] != e)

    def weight_copies(ex):
        return (pltpu.make_async_copy(wg_hbm.at[layer, ex], stage_g, sem.at[0]),
                pltpu.make_async_copy(wu_hbm.at[layer, ex], stage_u, sem.at[1]),
                pltpu.make_async_copy(wd_hbm.at[layer, ex], stage_d, sem.at[2]))

    @pl.when(b == 0)
    def _():
        for cp in weight_copies(e):
            cp.start()

    @pl.when(jnp.logical_and(used, first))
    def _():
        for cp in weight_copies(e):
            cp.wait()

        def cast(src, dst):
            def step(r, carry):
                rows = pl.ds(pl.multiple_of(r * CAST_ROWS, CAST_ROWS), CAST_ROWS)
                dst[rows, :] = src[rows, :].astype(BF16)
                return carry
            lax.fori_loop(0, src.shape[0] // CAST_ROWS, step, 0)

        cast(stage_g, wg_bf)
        cast(stage_u, wu_bf)
        cast(stage_d, wd_bf)

        @pl.when(nx_ref[b] >= 0)
        def _():
            for cp in weight_copies(nx_ref[b]):
                cp.start()

    @pl.when(used)
    def _():
        x = x_ref[...].astype(BF16)
        g = jnp.minimum(jnp.dot(x, wg_bf[...], preferred_element_type=F32) + bg_ref[0, 0], SWIGLU_LIMIT)
        u = jnp.clip(jnp.dot(x, wu_bf[...], preferred_element_type=F32) + bu_ref[0, 0], -SWIGLU_LIMIT, SWIGLU_LIMIT)
        act = (u + 1.0) * (g * jax.nn.sigmoid(SWIGLU_ALPHA * g))
        y_ref[...] = jnp.dot(act.astype(BF16), wd_bf[...], preferred_element_type=F32) + bd_ref[0, 0]

    @pl.when(jnp.logical_not(used))
    def _():
        y_ref[...] = jnp.zeros_like(y_ref)


def _experts(x_slots, block_expert, next_expert, n_used, layer, wg, bg, wu, bu, wd, bd):
    n_slots, d = x_slots.shape
    n_blocks = n_slots // EXPERT_TILE
    depth, n_exp, _, de = wg.shape

    def used_block(b, be, nx, nu):
        return (jnp.minimum(b, nu[0] - 1), 0)

    def bias_spec(width):
        return pl.BlockSpec((1, 1, 1, width), lambda b, be, nx, nu: (layer, be[b], 0, 0))

    hbm = pl.BlockSpec(memory_space=pl.ANY)
    grid_spec = pltpu.PrefetchScalarGridSpec(
        num_scalar_prefetch=3,
        grid=(n_blocks,),
        in_specs=[pl.BlockSpec((EXPERT_TILE, d), used_block), bias_spec(de), bias_spec(de), bias_spec(d), hbm, hbm, hbm],
        out_specs=pl.BlockSpec((EXPERT_TILE, d), lambda b, be, nx, nu: (b, 0)),
        scratch_shapes=[
            pltpu.VMEM((d, de), F32), pltpu.VMEM((d, de), F32), pltpu.VMEM((de, d), F32),
            pltpu.VMEM((d, de), BF16), pltpu.VMEM((d, de), BF16), pltpu.VMEM((de, d), BF16),
            pltpu.SemaphoreType.DMA((3,)),
        ],
    )
    return pl.pallas_call(
        functools.partial(_expert_kernel, layer=layer),
        grid_spec=grid_spec,
        out_shape=jax.ShapeDtypeStruct((n_slots, d), F32),
        compiler_params=_cparams(1),
        name="moe_experts",
    )(block_expert, next_expert, n_used, x_slots,
      bg.reshape(depth, n_exp, 1, de), bu.reshape(depth, n_exp, 1, de), bd.reshape(depth, n_exp, 1, d), wg, wu, wd)


COMBINE_TILE = 128


def _combine_kernel(dest_ref, y_hbm, gt_ref, x_ref, gate_ref, o_ref, buf, sem, *, tiles_per_batch, ctx_tiles, ctx_row):
    i = pl.program_id(0)

    def row_copy(t, k):
        d = dest_ref[t * TOP_K + k]
        return pltpu.make_async_copy(y_hbm.at[pl.ds(d, 1), :], buf.at[k, pl.ds(t, 1), :], sem)

    def start(t, carry):
        for k in range(TOP_K):
            row_copy(t, k).start(priority=k % 2)
        return carry

    def wait(t, carry):
        for k in range(TOP_K):
            row_copy(t, k).wait()
        return carry

    lax.fori_loop(0, COMBINE_TILE, start, 0)
    lax.fori_loop(0, COMBINE_TILE, wait, 0)

    gates = gt_ref[...]
    y = gates[:, 0:1] * buf[0]
    for k in range(1, TOP_K):
        y = y + gates[:, k:k + 1] * buf[k]
    b = i // tiles_per_batch
    is_ctx = (i % tiles_per_batch) < ctx_tiles
    mod_gate = jnp.where(is_ctx, gate_ref[ctx_row:ctx_row + 1, :], gate_ref[pl.ds(b, 1), :])
    o_ref[...] = x_ref[...] + mod_gate * y


def _combine(y_slots, dest_flat, gates, xs, mod, rows_per_batch, ctx_row):
    n_tok, d = xs.shape
    kern = functools.partial(_combine_kernel, tiles_per_batch=rows_per_batch // COMBINE_TILE,
                             ctx_tiles=CTX_LEN // COMBINE_TILE, ctx_row=ctx_row)
    return pl.pallas_call(
        kern,
        grid=(n_tok // COMBINE_TILE,),
        in_specs=[
            pl.BlockSpec((COMBINE_TILE * TOP_K,), lambda i: (i,), memory_space=pltpu.SMEM),
            pl.BlockSpec(memory_space=pl.ANY),
            pl.BlockSpec((COMBINE_TILE, LANES), lambda i: (i, 0)),
            pl.BlockSpec((COMBINE_TILE, d), lambda i: (i, 0)),
            pl.BlockSpec((MOD_ROWS, d), lambda i: (0, 5)),
        ],
        out_specs=pl.BlockSpec((COMBINE_TILE, d), lambda i: (i, 0)),
        out_shape=jax.ShapeDtypeStruct((n_tok, d), F32),
        scratch_shapes=[pltpu.VMEM((TOP_K, COMBINE_TILE, d), F32), pltpu.SemaphoreType.DMA(())],
        compiler_params=_cparams(1),
        name="moe_combine",
    )(dest_flat, y_slots, gates, xs, mod)


def _final_norm_kernel(x_ref, g_ref, o_ref):
    o_ref[0] = _rms(x_ref[0], g_ref[...])


def _final_norm(xs3, g):
    b, t, d = xs3.shape
    blk = CTX_LEN
    n_lat = t - CTX_LEN
    return pl.pallas_call(
        _final_norm_kernel,
        grid=(b, n_lat // blk),
        in_specs=[
            pl.BlockSpec((1, blk, d), lambda bi, j: (bi, j + 1, 0)),
            pl.BlockSpec((1, d), lambda bi, j: (0, 0)),
        ],
        out_specs=pl.BlockSpec((1, blk, d), lambda bi, j: (bi, j, 0)),
        out_shape=jax.ShapeDtypeStruct((b, n_lat, d), F32),
        compiler_params=_cparams(2),
        name="final_norm",
    )(xs3, g.reshape(1, d))


def _moe_layer(xs, g, mod, rows_per_batch, ctx_row, layer, router_w, router_b, wg, bg, wu, bu, wd, bd):
    n_tok, d = xs.shape
    tpb = rows_per_batch // ROW_TILE
    h, e_pad, r_pad, gates, cnt = _router(xs, g, mod, router_w, router_b, tpb, ctx_row)
    counts = cnt[0, :N_EXPERTS]
    padded = ((counts + EXPERT_TILE - 1) // EXPERT_TILE) * EXPERT_TILE
    pend = jnp.cumsum(padded)
    pstart = pend - padded
    n_blocks = (n_tok * TOP_K) // EXPERT_TILE + N_EXPERTS
    experts = jnp.arange(N_EXPERTS, dtype=jnp.int32)
    e_idx = e_pad[:, :TOP_K]
    dest = (jnp.sum(jnp.where(e_idx[..., None] == experts, pstart, 0), axis=-1) + r_pad[:, :TOP_K]).reshape(-1).astype(jnp.int32)
    block_row0 = jnp.arange(n_blocks, dtype=jnp.int32) * EXPERT_TILE
    block_expert = jnp.minimum(jnp.sum(pend[None, :] <= block_row0[:, None], axis=1), N_EXPERTS - 1).astype(jnp.int32)
    n_used = (pend[-1:] // EXPERT_TILE).astype(jnp.int32)
    later = jnp.logical_and(block_expert[None, :] > block_expert[:, None], (block_row0 < pend[-1])[None, :])
    next_expert = jnp.min(jnp.where(later, block_expert[None, :], N_EXPERTS), axis=1)
    next_expert = jnp.where(next_expert == N_EXPERTS, -1, next_expert).astype(jnp.int32)
    x_slots = _dispatch(h, dest, (pstart + counts).astype(jnp.int32), (padded - counts).astype(jnp.int32), n_used,
                        n_blocks * EXPERT_TILE)
    y_slots = _experts(x_slots, block_expert, next_expert, n_used, layer, wg, bg, wu, bu, wd, bd)
    return _combine(y_slots, dest, gates, xs, mod, rows_per_batch, ctx_row)


def kernel(x, c, ctx, c_ctx, ada_w, ada_b, norm_mix_g, norm_ffn_g, final_g, na_wqkv, na_wo, na_rpb, gqa_wqkv, gqa_wo, gqa_q_g, gqa_k_g, diff_wqkv, diff_wo, diff_lambda, diff_subln_g, router_w, router_b, w_gate, b_gate, w_up, b_up, w_down, b_down):
    B, S, D = x.shape
    L = ctx.shape[1]
    T = L + S
    assert D == D_MODEL and L == CTX_LEN and T % ROW_TILE == 0 and B < MOD_ROWS
    tpb = T // ROW_TILE
    ctx_row = B

    cvec = jnp.zeros((MOD_ROWS, D), F32).at[:B].set(c).at[B].set(c_ctx)
    mods = _ada_mod(cvec, ada_w, ada_b)
    rope = _rope_tables(S)

    xs = jnp.concatenate([ctx, x], axis=1).reshape(B * T, D)
    for i in range(DEPTH):
        need_ctx = i < DEPTH - 1
        mod = mods[i]
        kind, j = i % 3, i // 3
        if kind == 0:
            qkv = _norm_mm(xs, norm_mix_g[i], mod, na_wqkv[j].astype(BF16), tpb, ctx_row).reshape(B, T, -1)
            o = _na_attention(qkv, _na_bias_table(na_rpb[j]), need_ctx)
            w_o = na_wo[j]
        elif kind == 1:
            qkv = _norm_mm(xs, norm_mix_g[i], mod, gqa_wqkv[j].astype(BF16), tpb, ctx_row).reshape(B, T, -1)
            o = _gqa_attention(qkv, rope, gqa_q_g[j], gqa_k_g[j], need_ctx)
            w_o = gqa_wo[j]
        else:
            qkv = _norm_mm(xs, norm_mix_g[i], mod, diff_wqkv[j].astype(BF16), tpb, ctx_row).reshape(B, T, -1)
            o = _diff_attention(qkv, rope, diff_lambda[j], diff_subln_g[j], i, need_ctx)
            w_o = diff_wo[j]
        xs = _out_proj(o.reshape(B * T, D), w_o.astype(BF16), xs, mod, tpb, ctx_row)
        xs = _moe_layer(xs, norm_ffn_g[i], mod, T, ctx_row, i, router_w[i], router_b[i],
                        w_gate, b_gate, w_up, b_up, w_down, b_down)
    return _final_norm(xs.reshape(B, T, D), final_g)
```

```python
import functools
import math

import jax
import jax.numpy as jnp
from jax import lax
from jax.experimental import pallas as pl
from jax.experimental.pallas import tpu as pltpu

F32 = jnp.float32
BF16 = jnp.bfloat16

D_MODEL = 2048
DEPTH = 4
GRID_W = 64
CTX_LEN = 256
HEAD_DIM = 128
ROPE_THETA = 10000.0
NORM_EPS = 1e-6
MASK_VALUE = -1e30
NA_HEADS = 16
NA_WIN_H = 8
NA_WIN_W = 16
GQA_HEADS = 16
GQA_KV_HEADS = 4
DIFF_HEADS = 8
N_EXPERTS = 32
TOP_K = 4
D_EXPERT = 768
SWIGLU_LIMIT = 7.0
SWIGLU_ALPHA = 1.702

LANES = 128
ROW_TILE = 768
EXPERT_TILE = 256
MOD_ROWS = 8
VMEM_LIMIT = 56 * 1024 * 1024

_NT = (((1,), (1,)), ((), ()))
LOG2_E = math.log2(math.e)
SCORE_SCALE = HEAD_DIM ** -0.5 * LOG2_E


def _cparams(n_axes, vmem=VMEM_LIMIT):
    return pltpu.CompilerParams(dimension_semantics=("arbitrary",) * n_axes, vmem_limit_bytes=vmem)


def _row_mod(tile, tiles_per_batch, ctx_row, ref):
    b = tile // tiles_per_batch
    rows = lax.broadcasted_iota(jnp.int32, (ROW_TILE, 1), 0)
    is_ctx = jnp.logical_and(tile % tiles_per_batch == 0, rows < CTX_LEN)
    return jnp.where(is_ctx, ref[ctx_row:ctx_row + 1, :], ref[pl.ds(b, 1), :])


def _rms(x, g):
    return x * lax.rsqrt(jnp.mean(x * x, axis=-1, keepdims=True) + NORM_EPS) * g


def _pack_bf16_pairs(x):
    w = x.shape[1] // 2
    u = lax.bitcast_convert_type(x, jnp.int32)
    r = u + 0x7FFF + (lax.shift_right_logical(u, 16) & 1)
    return (r[:, w:] & -65536) | lax.shift_right_logical(r[:, :w], 16)


def _unpack_bf16_pairs(p):
    lo = lax.bitcast_convert_type(lax.shift_left(p, 16), F32)
    hi = lax.bitcast_convert_type(p & -65536, F32)
    return jnp.concatenate([lo, hi], axis=1)


def _softmax_parts(s):
    m = jnp.max(s, axis=-1, keepdims=True)
    p = jnp.exp2(s - m)
    return p, jnp.sum(p, axis=-1, keepdims=True)


def _ada_kernel(c_ref, w_ref, b_ref, o_ref):
    c = c_ref[...]
    s = c * jax.nn.sigmoid(c)
    o_ref[0] = jnp.dot(s.astype(BF16), w_ref[0].astype(BF16), preferred_element_type=F32) + b_ref[0]


def _ada_mod(cvec, ada_w, ada_b):
    depth, d, n = ada_w.shape
    tn = 1024
    return pl.pallas_call(
        _ada_kernel,
        grid=(depth, n // tn),
        in_specs=[
            pl.BlockSpec((MOD_ROWS, d), lambda l, j: (0, 0)),
            pl.BlockSpec((1, d, tn), lambda l, j: (l, 0, j)),
            pl.BlockSpec((1, 1, tn), lambda l, j: (l, 0, j)),
        ],
        out_specs=pl.BlockSpec((1, MOD_ROWS, tn), lambda l, j: (l, 0, j)),
        out_shape=jax.ShapeDtypeStruct((depth, MOD_ROWS, n), F32),
        compiler_params=_cparams(2),
        name="ada_mod",
    )(cvec, ada_w, ada_b.reshape(depth, 1, n))


def _norm_mm_kernel(x_ref, g_ref, sh_ref, sc_ref, w_ref, o_ref, h_ref, *, tpb, ctx_row):
    i = pl.program_id(0)

    @pl.when(pl.program_id(1) == 0)
    def _():
        y = _rms(x_ref[...], g_ref[...])
        sh = _row_mod(i, tpb, ctx_row, sh_ref)
        sc = _row_mod(i, tpb, ctx_row, sc_ref)
        h_ref[...] = (y * (1.0 + sc) + sh).astype(BF16)

    o_ref[...] = jnp.dot(h_ref[...], w_ref[...], preferred_element_type=F32).astype(BF16)


def _norm_mm(xs, g, mod, w_bf16, tpb, ctx_row):
    n_tok, d = xs.shape
    n_out = w_bf16.shape[1]
    tn = 1024
    kern = functools.partial(_norm_mm_kernel, tpb=tpb, ctx_row=ctx_row)
    return pl.pallas_call(
        kern,
        grid=(n_tok // ROW_TILE, n_out // tn),
        in_specs=[
            pl.BlockSpec((ROW_TILE, d), lambda i, j: (i, 0)),
            pl.BlockSpec((1, d), lambda i, j: (0, 0)),
            pl.BlockSpec((MOD_ROWS, d), lambda i, j: (0, 0)),
            pl.BlockSpec((MOD_ROWS, d), lambda i, j: (0, 1)),
            pl.BlockSpec((d, tn), lambda i, j: (0, j)),
        ],
        out_specs=pl.BlockSpec((ROW_TILE, tn), lambda i, j: (i, j)),
        out_shape=jax.ShapeDtypeStruct((n_tok, n_out), BF16),
        scratch_shapes=[pltpu.VMEM((ROW_TILE, d), BF16)],
        compiler_params=_cparams(2),
        name="norm_qkv",
    )(xs, g.reshape(1, d), mod, mod, w_bf16)


def _out_proj_kernel(o_ref, w_ref, x_ref, gate_ref, y_ref, *, tpb, ctx_row):
    i = pl.program_id(0)
    gate = _row_mod(i, tpb, ctx_row, gate_ref)
    y = jnp.dot(o_ref[...], w_ref[...], preferred_element_type=F32)
    y_ref[...] = x_ref[...] + gate * y


def _out_proj(o, w_bf16, xs, mod, tpb, ctx_row):
    n_tok, d = xs.shape
    tn = 1024
    kern = functools.partial(_out_proj_kernel, tpb=tpb, ctx_row=ctx_row)
    return pl.pallas_call(
        kern,
        grid=(n_tok // ROW_TILE, d // tn),
        in_specs=[
            pl.BlockSpec((ROW_TILE, d), lambda i, j: (i, 0)),
            pl.BlockSpec((d, tn), lambda i, j: (0, j)),
            pl.BlockSpec((ROW_TILE, tn), lambda i, j: (i, j)),
            pl.BlockSpec((MOD_ROWS, tn), lambda i, j: (0, 2 * (D_MODEL // tn) + j)),
        ],
        out_specs=pl.BlockSpec((ROW_TILE, tn), lambda i, j: (i, j)),
        out_shape=jax.ShapeDtypeStruct((n_tok, d), F32),
        compiler_params=_cparams(2),
        name="out_proj",
    )(o, w_bf16, xs, mod)


def _na_kernel(q_ref, k_ref, v_ref, tab_ref, o_ref, *, need_ctx, n_lat):
    scale = SCORE_SCALE
    L = CTX_LEN
    kc = k_ref[0, :L, :]
    vc = v_ref[0, :L, :]
    if need_ctx:
        s = lax.dot_general(q_ref[0, :L, :], kc, _NT, preferred_element_type=F32) * scale
        p, l = _softmax_parts(s)
        o = jnp.dot(p.astype(BF16), vc, preferred_element_type=F32) / l
        o_ref[0, :L, :] = o.astype(BF16)
    else:
        o_ref[0, :L, :] = jnp.zeros((L, HEAD_DIM), BF16)

    rows = n_lat // GRID_W
    qb_rows = 2
    win_rows = 10
    n_blocks = rows // qb_rows

    def body(mq, carry):
        q0 = pl.multiple_of(L + mq * (qb_rows * GRID_W), LANES)
        q = q_ref[0, pl.ds(q0, qb_rows * GRID_W), :]
        u0 = jnp.clip(qb_rows * mq - NA_WIN_H // 2, 0, rows - win_rows)
        k0 = pl.multiple_of(L + u0 * GRID_W, LANES)
        kw = k_ref[0, pl.ds(k0, win_rows * GRID_W), :]
        vw = v_ref[0, pl.ds(k0, win_rows * GRID_W), :]
        s_c = lax.dot_general(q, kc, _NT, preferred_element_type=F32) * scale
        s_l = lax.dot_general(q, kw, _NT, preferred_element_type=F32) * scale
        bias_rows = []
        for a in range(qb_rows):
            qr = qb_rows * mq + a
            r0q = jnp.clip(qr - NA_WIN_H // 2, 0, rows - NA_WIN_H)
            tiles = []
            for p in range(win_rows // 2):
                kr0 = u0 + 2 * p
                kr1 = kr0 + 1
                i0 = jnp.where(jnp.logical_and(kr0 >= r0q, kr0 < r0q + NA_WIN_H), kr0 - qr + NA_WIN_H - 1, 2 * NA_WIN_H - 1)
                i1 = jnp.where(jnp.logical_and(kr1 >= r0q, kr1 < r0q + NA_WIN_H), kr1 - qr + NA_WIN_H - 1, 2 * NA_WIN_H - 1)
                tiles.append(tab_ref[0, 0, i0] + tab_ref[0, 1, i1])
            bias_rows.append(jnp.concatenate(tiles, axis=1))
        s_l = s_l + jnp.concatenate(bias_rows, axis=0)
        m = jnp.maximum(jnp.max(s_c, axis=-1, keepdims=True), jnp.max(s_l, axis=-1, keepdims=True))
        p_c = jnp.exp2(s_c - m)
        p_l = jnp.exp2(s_l - m)
        l = jnp.sum(p_c, axis=-1, keepdims=True) + jnp.sum(p_l, axis=-1, keepdims=True)
        o = jnp.dot(p_c.astype(BF16), vc, preferred_element_type=F32) + jnp.dot(p_l.astype(BF16), vw, preferred_element_type=F32)
        o_ref[0, pl.ds(q0, qb_rows * GRID_W), :] = (o / l).astype(BF16)
        return carry

    lax.fori_loop(0, n_blocks, body, 0, unroll=2)


def _na_bias_table(rpb):
    col = jnp.arange(GRID_W)
    col_start = jnp.clip(col - NA_WIN_W // 2, 0, GRID_W - NA_WIN_W)
    col_mask = (col[None, :] >= col_start[:, None]) & (col[None, :] < col_start[:, None] + NA_WIN_W)
    dc_idx = jnp.clip(col[None, :] - col[:, None] + NA_WIN_W - 1, 0, 2 * NA_WIN_W - 2)
    tab = rpb[:, :, dc_idx] * LOG2_E
    tab = jnp.where(col_mask[None, None], tab, MASK_VALUE)
    tab = jnp.concatenate([tab, jnp.full_like(tab[:, :1], MASK_VALUE)], axis=1)
    zeros = jnp.zeros_like(tab)
    left = jnp.concatenate([tab, zeros], axis=-1)
    right = jnp.concatenate([zeros, tab], axis=-1)
    return jnp.stack([left, right], axis=1).astype(F32)


def _na_attention(qkv, tab, need_ctx):
    b, t, _ = qkv.shape
    h = NA_HEADS
    kern = functools.partial(_na_kernel, need_ctx=need_ctx, n_lat=t - CTX_LEN)
    return pl.pallas_call(
        kern,
        grid=(b, h),
        in_specs=[
            pl.BlockSpec((1, t, HEAD_DIM), lambda bi, hi: (bi, 0, hi)),
            pl.BlockSpec((1, t, HEAD_DIM), lambda bi, hi: (bi, 0, h + hi)),
            pl.BlockSpec((1, t, HEAD_DIM), lambda bi, hi: (bi, 0, 2 * h + hi)),
            pl.BlockSpec((1, 2, 2 * NA_WIN_H, GRID_W, LANES), lambda bi, hi: (hi, 0, 0, 0, 0)),
        ],
        out_specs=pl.BlockSpec((1, t, HEAD_DIM), lambda bi, hi: (bi, 0, hi)),
        out_shape=jax.ShapeDtypeStruct((b, t, h * HEAD_DIM), BF16),
        compiler_params=_cparams(2),
        name="na_attn",
    )(qkv, qkv, qkv, tab)


def _rope_tables(n_tok):
    t = jnp.arange(n_tok)
    row = (t // GRID_W).astype(F32)
    col = (t % GRID_W).astype(F32)
    half = HEAD_DIM // 2
    inv_freq = ROPE_THETA ** (-jnp.arange(0, half, 2, dtype=F32) / half)
    ang_r = row[:, None] * inv_freq[None, :]
    ang_c = col[:, None] * inv_freq[None, :]
    ang = jnp.concatenate([ang_r, ang_r, ang_c, ang_c], axis=-1)
    cos, sin = jnp.cos(ang), jnp.sin(ang)
    upper = (jnp.arange(HEAD_DIM) % half) >= (half // 2)
    sin_a = jnp.where(upper[None, :], sin, 0.0)
    sin_b = jnp.where(upper[None, :], 0.0, -sin)
    return cos, sin_a, sin_b


def _rope(x, cos, sin_a, sin_b):
    quarter = HEAD_DIM // 4
    return x * cos + pltpu.roll(x, quarter, 1) * sin_a + pltpu.roll(x, HEAD_DIM - quarter, 1) * sin_b


def _gqa_kernel(q_ref, k_ref, v_ref, cos_ref, sa_ref, sb_ref, qg_ref, kg_ref, o_ref, kn_ref, *, need_ctx, n_lat):
    L = CTX_LEN
    G = GQA_HEADS // GQA_KV_HEADS
    scale = SCORE_SCALE
    tq = 128

    kg = kg_ref[...]
    kn_ref[:L, :] = _rms(k_ref[0, :L, :].astype(F32), kg).astype(BF16)
    k_lat = _rms(k_ref[0, L:, :].astype(F32), kg)
    kn_ref[L:, :] = _rope(k_lat, cos_ref[...], sa_ref[...], sb_ref[...]).astype(BF16)

    qg = qg_ref[...] * scale

    if need_ctx:
        qs = [_rms(q_ref[0, :L, hh * HEAD_DIM:(hh + 1) * HEAD_DIM].astype(F32), qg).astype(BF16) for hh in range(G)]
        s = lax.dot_general(jnp.concatenate(qs, axis=0), kn_ref[:L, :], _NT, preferred_element_type=F32)
        p, l = _softmax_parts(s)
        o = jnp.dot(p.astype(BF16), v_ref[0, :L, :], preferred_element_type=F32) / l
        for hh in range(G):
            o_ref[0, :L, hh * HEAD_DIM:(hh + 1) * HEAD_DIM] = o[hh * L:(hh + 1) * L].astype(BF16)
    else:
        o_ref[0, :L, :] = jnp.zeros((L, G * HEAD_DIM), BF16)

    def body(mq, carry):
        r0 = pl.multiple_of(mq * tq, tq)
        q0 = pl.multiple_of(L + mq * tq, tq)
        cos = cos_ref[pl.ds(r0, tq), :]
        sa = sa_ref[pl.ds(r0, tq), :]
        sb = sb_ref[pl.ds(r0, tq), :]
        for hh in range(G):
            qn = _rms(q_ref[0, pl.ds(q0, tq), hh * HEAD_DIM:(hh + 1) * HEAD_DIM].astype(F32), qg)
            qh = _rope(qn, cos, sa, sb).astype(BF16)
            s = lax.dot_general(qh, kn_ref[...], _NT, preferred_element_type=F32)
            p, l = _softmax_parts(s)
            o = jnp.dot(p.astype(BF16), v_ref[0], preferred_element_type=F32) / l
            o_ref[0, pl.ds(q0, tq), hh * HEAD_DIM:(hh + 1) * HEAD_DIM] = o.astype(BF16)
        return carry

    lax.fori_loop(0, n_lat // tq, body, 0, unroll=2)


def _gqa_attention(qkv, rope, q_g, k_g, need_ctx):
    b, t, _ = qkv.shape
    n_lat = t - CTX_LEN
    G = GQA_HEADS // GQA_KV_HEADS
    gw = G * HEAD_DIM
    cos, sin_a, sin_b = rope
    kern = functools.partial(_gqa_kernel, need_ctx=need_ctx, n_lat=n_lat)
    tab_spec = pl.BlockSpec((n_lat, HEAD_DIM), lambda bi, gi: (0, 0))
    vec_spec = pl.BlockSpec((1, HEAD_DIM), lambda bi, gi: (0, 0))
    return pl.pallas_call(
        kern,
        grid=(b, GQA_KV_HEADS),
        in_specs=[
            pl.BlockSpec((1, t, gw), lambda bi, gi: (bi, 0, gi)),
            pl.BlockSpec((1, t, HEAD_DIM), lambda bi, gi: (bi, 0, GQA_HEADS + gi)),
            pl.BlockSpec((1, t, HEAD_DIM), lambda bi, gi: (bi, 0, GQA_HEADS + GQA_KV_HEADS + gi)),
            tab_spec, tab_spec, tab_spec, vec_spec, vec_spec,
        ],
        out_specs=pl.BlockSpec((1, t, gw), lambda bi, gi: (bi, 0, gi)),
        out_shape=jax.ShapeDtypeStruct((b, t, GQA_HEADS * HEAD_DIM), BF16),
        scratch_shapes=[pltpu.VMEM((t, HEAD_DIM), BF16)],
        compiler_params=_cparams(2),
        name="gqa_attn",
    )(qkv, qkv, qkv, cos, sin_a, sin_b, q_g.reshape(1, HEAD_DIM), k_g.reshape(1, HEAD_DIM))


def _diff_kernel(q_ref, k_ref, v_ref, cos_ref, sa_ref, sb_ref, lam_ref, g_ref, o_ref, kr_ref, *, need_ctx, n_lat, lambda_init):
    L = CTX_LEN
    scale = SCORE_SCALE
    tq = 128
    lam = lam_ref[...]
    lam_full = (jnp.exp(jnp.sum(lam[0:1] * lam[1:2], axis=-1, keepdims=True))
                - jnp.exp(jnp.sum(lam[2:3] * lam[3:4], axis=-1, keepdims=True)) + lambda_init)
    g = g_ref[...] * (1.0 - lambda_init)

    kr_ref[:L, :] = k_ref[0, :L, :]
    for c in range(2):
        k_lat = k_ref[0, L:, c * HEAD_DIM:(c + 1) * HEAD_DIM].astype(F32)
        kr_ref[L:, c * HEAD_DIM:(c + 1) * HEAD_DIM] = _rope(k_lat, cos_ref[...], sa_ref[...], sb_ref[...]).astype(BF16)

    def attend(qs, n_keys):
        ps = []
        for c in range(2):
            s = lax.dot_general(qs[c], kr_ref[:n_keys, c * HEAD_DIM:(c + 1) * HEAD_DIM], _NT, preferred_element_type=F32)
            p, l = _softmax_parts(s)
            ps.append(p * ((1.0 if c == 0 else lam_full) / l))
        a = (ps[0] - ps[1]).astype(BF16)
        o = jnp.dot(a, v_ref[0, :n_keys, :], preferred_element_type=F32)
        return _rms(o, g).astype(BF16)

    if need_ctx:
        qs = [(q_ref[0, :L, c * HEAD_DIM:(c + 1) * HEAD_DIM].astype(F32) * scale).astype(BF16) for c in range(2)]
        o_ref[0, :L, :] = attend(qs, L)
    else:
        o_ref[0, :L, :] = jnp.zeros((L, 2 * HEAD_DIM), BF16)

    def body(mq, carry):
        r0 = pl.multiple_of(mq * tq, tq)
        q0 = pl.multiple_of(L + mq * tq, tq)
        cos = cos_ref[pl.ds(r0, tq), :]
        sa = sa_ref[pl.ds(r0, tq), :]
        sb = sb_ref[pl.ds(r0, tq), :]
        qs = []
        for c in range(2):
            q = q_ref[0, pl.ds(q0, tq), c * HEAD_DIM:(c + 1) * HEAD_DIM].astype(F32) * scale
            qs.append(_rope(q, cos, sa, sb).astype(BF16))
        o_ref[0, pl.ds(q0, tq), :] = attend(qs, L + n_lat)
        return carry

    lax.fori_loop(0, n_lat // tq, body, 0, unroll=2)


def _diff_attention(qkv, rope, lam, subln_g, layer_idx, need_ctx):
    b, t, _ = qkv.shape
    n_lat = t - CTX_LEN
    hw = 2 * HEAD_DIM
    cos, sin_a, sin_b = rope
    lambda_init = 0.8 - 0.6 * math.exp(-0.3 * layer_idx)
    kern = functools.partial(_diff_kernel, need_ctx=need_ctx, n_lat=n_lat, lambda_init=lambda_init)
    tab_spec = pl.BlockSpec((n_lat, HEAD_DIM), lambda bi, hi: (0, 0))
    return pl.pallas_call(
        kern,
        grid=(b, DIFF_HEADS),
        in_specs=[
            pl.BlockSpec((1, t, hw), lambda bi, hi: (bi, 0, hi)),
            pl.BlockSpec((1, t, hw), lambda bi, hi: (bi, 0, DIFF_HEADS + hi)),
            pl.BlockSpec((1, t, hw), lambda bi, hi: (bi, 0, 2 * DIFF_HEADS + hi)),
            tab_spec, tab_spec, tab_spec,
            pl.BlockSpec((4, HEAD_DIM), lambda bi, hi: (0, 0)),
            pl.BlockSpec((1, hw), lambda bi, hi: (0, 0)),
        ],
        out_specs=pl.BlockSpec((1, t, hw), lambda bi, hi: (bi, 0, hi)),
        out_shape=jax.ShapeDtypeStruct((b, t, DIFF_HEADS * hw), BF16),
        scratch_shapes=[pltpu.VMEM((t, hw), BF16)],
        compiler_params=_cparams(2),
        name="diff_attn",
    )(qkv, qkv, qkv, cos, sin_a, sin_b, lam, subln_g.reshape(1, hw))


def _router_kernel(x_ref, g_ref, sh_ref, sc_ref, rw_ref, rb_ref, h_ref, e_ref, r_ref, gt_ref, cnt_ref, run_ref, *, tpb, ctx_row):
    i = pl.program_id(0)

    @pl.when(i == 0)
    def _():
        run_ref[...] = jnp.zeros_like(run_ref)

    y = _rms(x_ref[...], g_ref[...])
    h = y * (1.0 + _row_mod(i, tpb, ctx_row, sc_ref)) + _row_mod(i, tpb, ctx_row, sh_ref)
    h_ref[...] = _pack_bf16_pairs(h)

    h_hi = h.astype(BF16)
    h_lo = (h - h_hi.astype(F32)).astype(BF16)
    w = rw_ref[...]
    w_hi = w.astype(BF16)
    w_lo = (w - w_hi.astype(F32)).astype(BF16)
    logits = (jnp.dot(h_hi, w_hi, preferred_element_type=F32) + jnp.dot(h_lo, w_hi, preferred_element_type=F32)
              + jnp.dot(h_hi, w_lo, preferred_element_type=F32)) + rb_ref[...]

    lane = lax.broadcasted_iota(jnp.int32, (ROW_TILE, LANES), 1)
    lane_f = lane.astype(F32)
    work = jnp.where(lane < N_EXPERTS, logits, -jnp.inf)
    vals, onehots = [], []
    for _ in range(TOP_K):
        m = jnp.max(work, axis=-1, keepdims=True)
        idx = jnp.min(jnp.where(work == m, lane_f, float(LANES)), axis=-1, keepdims=True)
        hit = lane_f == idx
        vals.append(m)
        onehots.append(hit)
        work = jnp.where(hit, -jnp.inf, work)

    exps = [jnp.exp(v - vals[0]) for v in vals]
    denom = exps[0] + exps[1] + exps[2] + exps[3]

    member = onehots[0]
    for k in range(1, TOP_K):
        member = jnp.logical_or(member, onehots[k])
    member_f = member.astype(F32)
    r_i = lax.broadcasted_iota(jnp.int32, (ROW_TILE, ROW_TILE), 0)
    c_i = lax.broadcasted_iota(jnp.int32, (ROW_TILE, ROW_TILE), 1)
    strict_lower = (c_i < r_i).astype(BF16)
    before = jnp.dot(strict_lower, member_f.astype(BF16), preferred_element_type=F32) + run_ref[...]

    e_out = jnp.zeros((ROW_TILE, LANES), F32)
    r_out = jnp.zeros((ROW_TILE, LANES), F32)
    g_out = jnp.zeros((ROW_TILE, LANES), F32)
    for k in range(TOP_K):
        hit = onehots[k]
        e_k = jnp.sum(jnp.where(hit, lane_f, 0.0), axis=-1, keepdims=True)
        r_k = jnp.sum(jnp.where(hit, before, 0.0), axis=-1, keepdims=True)
        e_out = jnp.where(lane == k, e_k, e_out)
        r_out = jnp.where(lane == k, r_k, r_out)
        g_out = jnp.where(lane == k, exps[k] / denom, g_out)
    e_ref[...] = e_out.astype(jnp.int32)
    r_ref[...] = r_out.astype(jnp.int32)
    gt_ref[...] = g_out

    run_ref[...] = run_ref[...] + jnp.sum(member_f, axis=0, keepdims=True)
    cnt_ref[...] = run_ref[...].astype(jnp.int32)


def _router(xs, g, mod, router_w, router_b, tpb, ctx_row):
    n_tok, d = xs.shape
    rw = jnp.zeros((d, LANES), F32).at[:, :N_EXPERTS].set(router_w)
    rb = jnp.zeros((1, LANES), F32).at[0, :N_EXPERTS].set(router_b)
    kern = functools.partial(_router_kernel, tpb=tpb, ctx_row=ctx_row)
    row_spec = pl.BlockSpec((ROW_TILE, LANES), lambda i: (i, 0))
    return pl.pallas_call(
        kern,
        grid=(n_tok // ROW_TILE,),
        in_specs=[
            pl.BlockSpec((ROW_TILE, d), lambda i: (i, 0)),
            pl.BlockSpec((1, d), lambda i: (0, 0)),
            pl.BlockSpec((MOD_ROWS, d), lambda i: (0, 3)),
            pl.BlockSpec((MOD_ROWS, d), lambda i: (0, 4)),
            pl.BlockSpec((d, LANES), lambda i: (0, 0)),
            pl.BlockSpec((1, LANES), lambda i: (0, 0)),
        ],
        out_specs=[
            pl.BlockSpec((ROW_TILE, d // 2), lambda i: (i, 0)),
            row_spec, row_spec, row_spec,
            pl.BlockSpec((1, LANES), lambda i: (0, 0)),
        ],
        out_shape=[
            jax.ShapeDtypeStruct((n_tok, d // 2), jnp.int32),
            jax.ShapeDtypeStruct((n_tok, LANES), jnp.int32),
            jax.ShapeDtypeStruct((n_tok, LANES), jnp.int32),
            jax.ShapeDtypeStruct((n_tok, LANES), F32),
            jax.ShapeDtypeStruct((1, LANES), jnp.int32),
        ],
        scratch_shapes=[pltpu.VMEM((1, LANES), F32)],
        compiler_params=_cparams(1),
        name="moe_router",
    )(xs, g.reshape(1, d), mod, mod, rw, rb)


DISPATCH_TILE = 512


def _dispatch_kernel(zero_blocks_ref, dest_ref, h_ref, slots_ref, zero_ref, sem, zero_sem):
    @pl.when(pl.program_id(0) == 0)
    def _():
        zero_ref[...] = jnp.zeros_like(zero_ref)

        def zero_copy(j):
            rows = pl.ds(pl.multiple_of(zero_blocks_ref[j] * EXPERT_TILE, EXPERT_TILE), EXPERT_TILE)
            return pltpu.make_async_copy(zero_ref, slots_ref.at[rows, :], zero_sem)

        def for_each_zero_block(fn):
            def body(j, carry):
                @pl.when(zero_blocks_ref[j] >= 0)
                def _():
                    fn(j)
                return carry
            lax.fori_loop(0, zero_blocks_ref.shape[0], body, 0)

        for_each_zero_block(lambda j: zero_copy(j).start())
        for_each_zero_block(lambda j: zero_copy(j).wait())

    def row_copy(t, k):
        d = dest_ref[t * TOP_K + k]
        return pltpu.make_async_copy(h_ref.at[pl.ds(t, 1), :], slots_ref.at[pl.ds(d, 1), :], sem)

    def start(t, carry):
        for k in range(TOP_K):
            row_copy(t, k).start(priority=k % 2)
        return carry

    def wait(t, carry):
        for k in range(TOP_K):
            row_copy(t, k).wait()
        return carry

    lax.fori_loop(0, DISPATCH_TILE, start, 0)
    lax.fori_loop(0, DISPATCH_TILE, wait, 0)


def _dispatch(h_packed, dest_flat, zero_blocks, n_slots):
    n_tok, w = h_packed.shape
    grid_spec = pltpu.PrefetchScalarGridSpec(
        num_scalar_prefetch=1,
        grid=(n_tok // DISPATCH_TILE,),
        in_specs=[
            pl.BlockSpec((DISPATCH_TILE * TOP_K,), lambda i, zb: (i,), memory_space=pltpu.SMEM),
            pl.BlockSpec((DISPATCH_TILE, w), lambda i, zb: (i, 0)),
        ],
        out_specs=pl.BlockSpec(memory_space=pl.ANY),
        scratch_shapes=[pltpu.VMEM((EXPERT_TILE, w), jnp.int32), pltpu.SemaphoreType.DMA(()), pltpu.SemaphoreType.DMA(())],
    )
    return pl.pallas_call(
        _dispatch_kernel,
        grid_spec=grid_spec,
        out_shape=jax.ShapeDtypeStruct((n_slots, w), jnp.int32),
        compiler_params=_cparams(1),
        name="moe_dispatch",
    )(zero_blocks, dest_flat, h_packed)


CAST_ROWS = 256


def _expert_kernel(be_ref, nx_ref, nu_ref, x_ref, bg_ref, bu_ref, bd_ref, wg_hbm, wu_hbm, wd_hbm, y_ref,
                   stage_g, stage_u, stage_d, wg_bf, wu_bf, wd_bf, sem, *, layer):
    b = pl.program_id(0)
    used = b < nu_ref[0]
    e = be_ref[b]
    first = jnp.logical_or(b == 0, be_ref[jnp.maximum(b - 1, 0)] != e)

    def weight_copies(ex):
        return (pltpu.make_async_copy(wg_hbm.at[layer, ex], stage_g, sem.at[0]),
                pltpu.make_async_copy(wu_hbm.at[layer, ex], stage_u, sem.at[1]),
                pltpu.make_async_copy(wd_hbm.at[layer, ex], stage_d, sem.at[2]))

    @pl.when(b == 0)
    def _():
        for cp in weight_copies(e):
            cp.start()

    @pl.when(jnp.logical_and(used, first))
    def _():
        for cp in weight_copies(e):
            cp.wait()

        def cast(src, dst):
            def step(r, carry):
                rows = pl.ds(pl.multiple_of(r * CAST_ROWS, CAST_ROWS), CAST_ROWS)
                dst[rows, :] = src[rows, :].astype(BF16)
                return carry
            lax.fori_loop(0, src.shape[0] // CAST_ROWS, step, 0)

        cast(stage_g, wg_bf)
        cast(stage_u, wu_bf)
        cast(stage_d, wd_bf)

        @pl.when(nx_ref[b] >= 0)
        def _():
            for cp in weight_copies(nx_ref[b]):
                cp.start()

    @pl.when(used)
    def _():
        x = _unpack_bf16_pairs(x_ref[...]).astype(BF16)
        g = jnp.minimum(jnp.dot(x, wg_bf[...], preferred_element_type=F32) + bg_ref[0, 0], SWIGLU_LIMIT)
        u = jnp.clip(jnp.dot(x, wu_bf[...], preferred_element_type=F32) + bu_ref[0, 0], -SWIGLU_LIMIT, SWIGLU_LIMIT)
        act = (u + 1.0) * (g * jax.nn.sigmoid(SWIGLU_ALPHA * g))
        y_ref[...] = _pack_bf16_pairs(jnp.dot(act.astype(BF16), wd_bf[...], preferred_element_type=F32) + bd_ref[0, 0])

    @pl.when(jnp.logical_not(used))
    def _():
        y_ref[...] = jnp.zeros_like(y_ref)


def _experts(x_slots, block_expert, next_expert, n_used, layer, wg, bg, wu, bu, wd, bd):
    n_slots, w = x_slots.shape
    n_blocks = n_slots // EXPERT_TILE
    depth, n_exp, d, de = wg.shape

    def used_block(b, be, nx, nu):
        return (jnp.minimum(b, nu[0] - 1), 0)

    def bias_spec(width):
        return pl.BlockSpec((1, 1, 1, width), lambda b, be, nx, nu: (layer, be[b], 0, 0))

    hbm = pl.BlockSpec(memory_space=pl.ANY)
    grid_spec = pltpu.PrefetchScalarGridSpec(
        num_scalar_prefetch=3,
        grid=(n_blocks,),
        in_specs=[pl.BlockSpec((EXPERT_TILE, w), used_block), bias_spec(de), bias_spec(de), bias_spec(d), hbm, hbm, hbm],
        out_specs=pl.BlockSpec((EXPERT_TILE, w), lambda b, be, nx, nu: (b, 0)),
        scratch_shapes=[
            pltpu.VMEM((d, de), F32), pltpu.VMEM((d, de), F32), pltpu.VMEM((de, d), F32),
            pltpu.VMEM((d, de), BF16), pltpu.VMEM((d, de), BF16), pltpu.VMEM((de, d), BF16),
            pltpu.SemaphoreType.DMA((3,)),
        ],
    )
    return pl.pallas_call(
        functools.partial(_expert_kernel, layer=layer),
        grid_spec=grid_spec,
        out_shape=jax.ShapeDtypeStruct((n_slots, w), jnp.int32),
        compiler_params=_cparams(1),
        name="moe_experts",
    )(block_expert, next_expert, n_used, x_slots,
      bg.reshape(depth, n_exp, 1, de), bu.reshape(depth, n_exp, 1, de), bd.reshape(depth, n_exp, 1, d), wg, wu, wd)


COMBINE_TILE = 256


def _combine_kernel(dest_ref, dest_next_ref, y_hbm, gt_ref, x_ref, gate_ref, o_ref, buf, sem, *, tiles_per_batch, ctx_tiles, ctx_row):
    i = pl.program_id(0)
    n = pl.num_programs(0)
    slot = i % 2

    def row_copy(dref, s, t, k):
        return pltpu.make_async_copy(y_hbm.at[pl.ds(dref[t * TOP_K + k], 1), :], buf.at[s, k, pl.ds(t, 1), :], sem.at[s])

    def start_tile(dref, s):
        def body(t, carry):
            for k in range(TOP_K):
                row_copy(dref, s, t, k).start(priority=k % 2)
            return carry
        lax.fori_loop(0, COMBINE_TILE, body, 0)

    @pl.when(i == 0)
    def _():
        start_tile(dest_ref, 0)

    @pl.when(i + 1 < n)
    def _():
        start_tile(dest_next_ref, 1 - slot)

    def wait_body(t, carry):
        for k in range(TOP_K):
            row_copy(dest_ref, slot, t, k).wait()
        return carry
    lax.fori_loop(0, COMBINE_TILE, wait_body, 0)

    gates = gt_ref[...]
    y = gates[:, 0:1] * _unpack_bf16_pairs(buf[slot, 0])
    for k in range(1, TOP_K):
        y = y + gates[:, k:k + 1] * _unpack_bf16_pairs(buf[slot, k])
    b = i // tiles_per_batch
    is_ctx = (i % tiles_per_batch) < ctx_tiles
    mod_gate = jnp.where(is_ctx, gate_ref[ctx_row:ctx_row + 1, :], gate_ref[pl.ds(b, 1), :])
    o_ref[...] = x_ref[...] + mod_gate * y


def _combine(y_slots, dest_flat, gates, xs, mod, rows_per_batch, ctx_row):
    n_tok, d = xs.shape
    w = y_slots.shape[1]
    n_tiles = n_tok // COMBINE_TILE
    kern = functools.partial(_combine_kernel, tiles_per_batch=rows_per_batch // COMBINE_TILE,
                             ctx_tiles=CTX_LEN // COMBINE_TILE, ctx_row=ctx_row)
    return pl.pallas_call(
        kern,
        grid=(n_tiles,),
        in_specs=[
            pl.BlockSpec((COMBINE_TILE * TOP_K,), lambda i: (i,), memory_space=pltpu.SMEM),
            pl.BlockSpec((COMBINE_TILE * TOP_K,), lambda i: (jnp.minimum(i + 1, n_tiles - 1),), memory_space=pltpu.SMEM),
            pl.BlockSpec(memory_space=pl.ANY),
            pl.BlockSpec((COMBINE_TILE, LANES), lambda i: (i, 0)),
            pl.BlockSpec((COMBINE_TILE, d), lambda i: (i, 0)),
            pl.BlockSpec((MOD_ROWS, d), lambda i: (0, 5)),
        ],
        out_specs=pl.BlockSpec((COMBINE_TILE, d), lambda i: (i, 0)),
        out_shape=jax.ShapeDtypeStruct((n_tok, d), F32),
        scratch_shapes=[pltpu.VMEM((2, TOP_K, COMBINE_TILE, w), jnp.int32), pltpu.SemaphoreType.DMA((2,))],
        compiler_params=_cparams(1),
        name="moe_combine",
    )(dest_flat, dest_flat, y_slots, gates, xs, mod)


def _final_norm_kernel(x_ref, g_ref, o_ref):
    o_ref[0] = _rms(x_ref[0], g_ref[...])


def _final_norm(xs3, g):
    b, t, d = xs3.shape
    blk = CTX_LEN
    n_lat = t - CTX_LEN
    return pl.pallas_call(
        _final_norm_kernel,
        grid=(b, n_lat // blk),
        in_specs=[
            pl.BlockSpec((1, blk, d), lambda bi, j: (bi, j + 1, 0)),
            pl.BlockSpec((1, d), lambda bi, j: (0, 0)),
        ],
        out_specs=pl.BlockSpec((1, blk, d), lambda bi, j: (bi, j, 0)),
        out_shape=jax.ShapeDtypeStruct((b, n_lat, d), F32),
        compiler_params=_cparams(2),
        name="final_norm",
    )(xs3, g.reshape(1, d))


def _moe_layer(xs, g, mod, rows_per_batch, ctx_row, layer, router_w, router_b, wg, bg, wu, bu, wd, bd):
    n_tok, d = xs.shape
    tpb = rows_per_batch // ROW_TILE
    h, e_pad, r_pad, gates, cnt = _router(xs, g, mod, router_w, router_b, tpb, ctx_row)
    counts = cnt[0, :N_EXPERTS]
    padded = ((counts + EXPERT_TILE - 1) // EXPERT_TILE) * EXPERT_TILE
    pend = jnp.cumsum(padded)
    pstart = pend - padded
    n_blocks = (n_tok * TOP_K) // EXPERT_TILE + N_EXPERTS
    experts = jnp.arange(N_EXPERTS, dtype=jnp.int32)
    e_idx = e_pad[:, :TOP_K]
    dest = (jnp.sum(jnp.where(e_idx[..., None] == experts, pstart, 0), axis=-1) + r_pad[:, :TOP_K]).reshape(-1).astype(jnp.int32)
    block_row0 = jnp.arange(n_blocks, dtype=jnp.int32) * EXPERT_TILE
    block_expert = jnp.minimum(jnp.sum(pend[None, :] <= block_row0[:, None], axis=1), N_EXPERTS - 1).astype(jnp.int32)
    n_used = (pend[-1:] // EXPERT_TILE).astype(jnp.int32)
    later = jnp.logical_and(block_expert[None, :] > block_expert[:, None], (block_row0 < pend[-1])[None, :])
    next_expert = jnp.min(jnp.where(later, block_expert[None, :], N_EXPERTS), axis=1)
    next_expert = jnp.where(next_expert == N_EXPERTS, -1, next_expert).astype(jnp.int32)
    last_block = jnp.where(padded > counts, pend // EXPERT_TILE - 1, -1)
    tail_block = n_used[0] + experts
    zero_blocks = jnp.concatenate([last_block, jnp.where(tail_block < n_blocks, tail_block, -1)]).astype(jnp.int32)
    x_slots = _dispatch(h, dest, zero_blocks, n_blocks * EXPERT_TILE)
    y_slots = _experts(x_slots, block_expert, next_expert, n_used, layer, wg, bg, wu, bu, wd, bd)
    return _combine(y_slots, dest, gates, xs, mod, rows_per_batch, ctx_row)


def kernel(x, c, ctx, c_ctx, ada_w, ada_b, norm_mix_g, norm_ffn_g, final_g, na_wqkv, na_wo, na_rpb, gqa_wqkv, gqa_wo, gqa_q_g, gqa_k_g, diff_wqkv, diff_wo, diff_lambda, diff_subln_g, router_w, router_b, w_gate, b_gate, w_up, b_up, w_down, b_down):
    B, S, D = x.shape
    L = ctx.shape[1]
    T = L + S
    assert D == D_MODEL and L == CTX_LEN and T % ROW_TILE == 0 and B < MOD_ROWS
    tpb = T // ROW_TILE
    ctx_row = B

    cvec = jnp.zeros((MOD_ROWS, D), F32).at[:B].set(c).at[B].set(c_ctx)
    mods = _ada_mod(cvec, ada_w, ada_b)
    rope = _rope_tables(S)

    xs = jnp.concatenate([ctx, x], axis=1).reshape(B * T, D)
    for i in range(DEPTH):
        need_ctx = i < DEPTH - 1
        mod = mods[i]
        kind, j = i % 3, i // 3
        if kind == 0:
            qkv = _norm_mm(xs, norm_mix_g[i], mod, na_wqkv[j].astype(BF16), tpb, ctx_row).reshape(B, T, -1)
            o = _na_attention(qkv, _na_bias_table(na_rpb[j]), need_ctx)
            w_o = na_wo[j]
        elif kind == 1:
            qkv = _norm_mm(xs, norm_mix_g[i], mod, gqa_wqkv[j].astype(BF16), tpb, ctx_row).reshape(B, T, -1)
            o = _gqa_attention(qkv, rope, gqa_q_g[j], gqa_k_g[j], need_ctx)
            w_o = gqa_wo[j]
        else:
            qkv = _norm_mm(xs, norm_mix_g[i], mod, diff_wqkv[j].astype(BF16), tpb, ctx_row).reshape(B, T, -1)
            o = _diff_attention(qkv, rope, diff_lambda[j], diff_subln_g[j], i, need_ctx)
            w_o = diff_wo[j]
        xs = _out_proj(o.reshape(B * T, D), w_o.astype(BF16), xs, mod, tpb, ctx_row)
        xs = _moe_layer(xs, norm_ffn_g[i], mod, T, ctx_row, i, router_w[i], router_b[i],
                        w_gate, b_gate, w_up, b_up, w_down, b_down)
    return _final_norm(xs.reshape(B, T, D), final_g)
```

```python
import functools
import math

import jax
import jax.numpy as jnp
from jax import lax
from jax.experimental import pallas as pl
from jax.experimental.pallas import tpu as pltpu

F32 = jnp.float32
BF16 = jnp.bfloat16

D_MODEL = 2048
DEPTH = 4
GRID_W = 64
CTX_LEN = 256
HEAD_DIM = 128
ROPE_THETA = 10000.0
NORM_EPS = 1e-6
MASK_VALUE = -1e30
NA_HEADS = 16
NA_WIN_H = 8
NA_WIN_W = 16
GQA_HEADS = 16
GQA_KV_HEADS = 4
DIFF_HEADS = 8
N_EXPERTS = 32
TOP_K = 4
D_EXPERT = 768
SWIGLU_LIMIT = 7.0
SWIGLU_ALPHA = 1.702

LANES = 128
ROW_TILE = 768
EXPERT_TILE = 256
MOD_ROWS = 8
VMEM_LIMIT = 56 * 1024 * 1024

_NT = (((1,), (1,)), ((), ()))
LOG2_E = math.log2(math.e)
SCORE_SCALE = HEAD_DIM ** -0.5 * LOG2_E


def _cparams(n_axes, vmem=VMEM_LIMIT):
    return pltpu.CompilerParams(dimension_semantics=("arbitrary",) * n_axes, vmem_limit_bytes=vmem)


def _row_mod(tile, tiles_per_batch, ctx_row, ref):
    b = tile // tiles_per_batch
    rows = lax.broadcasted_iota(jnp.int32, (ROW_TILE, 1), 0)
    is_ctx = jnp.logical_and(tile % tiles_per_batch == 0, rows < CTX_LEN)
    return jnp.where(is_ctx, ref[ctx_row:ctx_row + 1, :], ref[pl.ds(b, 1), :])


def _rms(x, g):
    return x * lax.rsqrt(jnp.mean(x * x, axis=-1, keepdims=True) + NORM_EPS) * g


def _pack_bf16_pairs(x):
    w = x.shape[1] // 2
    u = lax.bitcast_convert_type(x, jnp.int32)
    r = u + 0x7FFF + (lax.shift_right_logical(u, 16) & 1)
    return (r[:, w:] & -65536) | lax.shift_right_logical(r[:, :w], 16)


def _unpack_bf16_pairs(p):
    lo = lax.bitcast_convert_type(lax.shift_left(p, 16), F32)
    hi = lax.bitcast_convert_type(p & -65536, F32)
    return jnp.concatenate([lo, hi], axis=1)


def _ones_augmented(v):
    return jnp.concatenate([v, jnp.ones_like(v)], axis=1)


def _attend_augmented(q, k, v_aug):
    s = lax.dot_general(q, k, _NT, preferred_element_type=F32)
    p = jnp.exp2((s - jnp.max(s, axis=-1, keepdims=True)).astype(BF16))
    o = jnp.dot(p, v_aug, preferred_element_type=F32)
    d = o.shape[1] // 2
    return o[:, :d] / o[:, d:d + 1]


def _softmax_parts(s):
    m = jnp.max(s, axis=-1, keepdims=True)
    p = jnp.exp2(s - m)
    return p, jnp.sum(p, axis=-1, keepdims=True)


def _ada_kernel(c_ref, w_ref, b_ref, o_ref):
    c = c_ref[...]
    s = c * jax.nn.sigmoid(c)
    o_ref[0] = jnp.dot(s.astype(BF16), w_ref[0].astype(BF16), preferred_element_type=F32) + b_ref[0]


def _ada_mod(cvec, ada_w, ada_b):
    depth, d, n = ada_w.shape
    tn = 1024
    return pl.pallas_call(
        _ada_kernel,
        grid=(depth, n // tn),
        in_specs=[
            pl.BlockSpec((MOD_ROWS, d), lambda l, j: (0, 0)),
            pl.BlockSpec((1, d, tn), lambda l, j: (l, 0, j)),
            pl.BlockSpec((1, 1, tn), lambda l, j: (l, 0, j)),
        ],
        out_specs=pl.BlockSpec((1, MOD_ROWS, tn), lambda l, j: (l, 0, j)),
        out_shape=jax.ShapeDtypeStruct((depth, MOD_ROWS, n), F32),
        compiler_params=_cparams(2),
        name="ada_mod",
    )(cvec, ada_w, ada_b.reshape(depth, 1, n))


def _norm_mm_kernel(x_ref, g_ref, sh_ref, sc_ref, w_ref, o_ref, h_ref, *, tpb, ctx_row):
    i = pl.program_id(0)

    @pl.when(pl.program_id(1) == 0)
    def _():
        y = _rms(x_ref[...], g_ref[...])
        sh = _row_mod(i, tpb, ctx_row, sh_ref)
        sc = _row_mod(i, tpb, ctx_row, sc_ref)
        h_ref[...] = (y * (1.0 + sc) + sh).astype(BF16)

    o_ref[...] = jnp.dot(h_ref[...], w_ref[...], preferred_element_type=F32).astype(BF16)


def _norm_mm(xs, g, mod, w_bf16, tpb, ctx_row):
    n_tok, d = xs.shape
    n_out = w_bf16.shape[1]
    tn = 1024
    kern = functools.partial(_norm_mm_kernel, tpb=tpb, ctx_row=ctx_row)
    return pl.pallas_call(
        kern,
        grid=(n_tok // ROW_TILE, n_out // tn),
        in_specs=[
            pl.BlockSpec((ROW_TILE, d), lambda i, j: (i, 0)),
            pl.BlockSpec((1, d), lambda i, j: (0, 0)),
            pl.BlockSpec((MOD_ROWS, d), lambda i, j: (0, 0)),
            pl.BlockSpec((MOD_ROWS, d), lambda i, j: (0, 1)),
            pl.BlockSpec((d, tn), lambda i, j: (0, j)),
        ],
        out_specs=pl.BlockSpec((ROW_TILE, tn), lambda i, j: (i, j)),
        out_shape=jax.ShapeDtypeStruct((n_tok, n_out), BF16),
        scratch_shapes=[pltpu.VMEM((ROW_TILE, d), BF16)],
        compiler_params=_cparams(2),
        name="norm_qkv",
    )(xs, g.reshape(1, d), mod, mod, w_bf16)


def _out_proj_kernel(o_ref, w_ref, x_ref, gate_ref, y_ref, *, tpb, ctx_row):
    i = pl.program_id(0)
    gate = _row_mod(i, tpb, ctx_row, gate_ref)
    y = jnp.dot(o_ref[...], w_ref[...], preferred_element_type=F32)
    y_ref[...] = x_ref[...] + gate * y


def _out_proj(o, w_bf16, xs, mod, tpb, ctx_row):
    n_tok, d = xs.shape
    tn = 1024
    kern = functools.partial(_out_proj_kernel, tpb=tpb, ctx_row=ctx_row)
    return pl.pallas_call(
        kern,
        grid=(n_tok // ROW_TILE, d // tn),
        in_specs=[
            pl.BlockSpec((ROW_TILE, d), lambda i, j: (i, 0)),
            pl.BlockSpec((d, tn), lambda i, j: (0, j)),
            pl.BlockSpec((ROW_TILE, tn), lambda i, j: (i, j)),
            pl.BlockSpec((MOD_ROWS, tn), lambda i, j: (0, 2 * (D_MODEL // tn) + j)),
        ],
        out_specs=pl.BlockSpec((ROW_TILE, tn), lambda i, j: (i, j)),
        out_shape=jax.ShapeDtypeStruct((n_tok, d), F32),
        compiler_params=_cparams(2),
        name="out_proj",
    )(o, w_bf16, xs, mod)


def _na_kernel(q_ref, k_ref, v_ref, tab_ref, o_ref, va_ref, *, need_ctx, n_lat):
    L = CTX_LEN
    va_ref[...] = _ones_augmented(v_ref[0])
    kc = k_ref[0, :L, :]
    vc = va_ref[:L, :]
    if need_ctx:
        o_ref[0, :L, :] = _attend_augmented(q_ref[0, :L, :], kc, vc).astype(BF16)
    else:
        o_ref[0, :L, :] = jnp.zeros((L, HEAD_DIM), BF16)

    rows = n_lat // GRID_W
    qb_rows = 2
    win_rows = 10
    n_blocks = rows // qb_rows

    def body(mq, carry):
        q0 = pl.multiple_of(L + mq * (qb_rows * GRID_W), LANES)
        q = q_ref[0, pl.ds(q0, qb_rows * GRID_W), :]
        u0 = jnp.clip(qb_rows * mq - NA_WIN_H // 2, 0, rows - win_rows)
        k0 = pl.multiple_of(L + u0 * GRID_W, LANES)
        kw = k_ref[0, pl.ds(k0, win_rows * GRID_W), :]
        vw = va_ref[pl.ds(k0, win_rows * GRID_W), :]
        s_c = lax.dot_general(q, kc, _NT, preferred_element_type=F32)
        s_l = lax.dot_general(q, kw, _NT, preferred_element_type=F32)
        bias_rows = []
        for a in range(qb_rows):
            qr = qb_rows * mq + a
            r0q = jnp.clip(qr - NA_WIN_H // 2, 0, rows - NA_WIN_H)
            tiles = []
            for p in range(win_rows // 2):
                kr0 = u0 + 2 * p
                kr1 = kr0 + 1
                i0 = jnp.where(jnp.logical_and(kr0 >= r0q, kr0 < r0q + NA_WIN_H), kr0 - qr + NA_WIN_H - 1, 2 * NA_WIN_H - 1)
                i1 = jnp.where(jnp.logical_and(kr1 >= r0q, kr1 < r0q + NA_WIN_H), kr1 - qr + NA_WIN_H - 1, 2 * NA_WIN_H - 1)
                tiles.append(tab_ref[0, 0, i0] + tab_ref[0, 1, i1])
            bias_rows.append(jnp.concatenate(tiles, axis=1))
        s_l = s_l + jnp.concatenate(bias_rows, axis=0)
        m = jnp.maximum(jnp.max(s_c, axis=-1, keepdims=True), jnp.max(s_l, axis=-1, keepdims=True))
        p_c = jnp.exp2((s_c - m).astype(BF16))
        p_l = jnp.exp2((s_l - m).astype(BF16))
        o = jnp.dot(p_c, vc, preferred_element_type=F32) + jnp.dot(p_l, vw, preferred_element_type=F32)
        o_ref[0, pl.ds(q0, qb_rows * GRID_W), :] = (o[:, :HEAD_DIM] / o[:, HEAD_DIM:HEAD_DIM + 1]).astype(BF16)
        return carry

    lax.fori_loop(0, n_blocks, body, 0, unroll=8)


def _na_bias_table(rpb):
    col = jnp.arange(GRID_W)
    col_start = jnp.clip(col - NA_WIN_W // 2, 0, GRID_W - NA_WIN_W)
    col_mask = (col[None, :] >= col_start[:, None]) & (col[None, :] < col_start[:, None] + NA_WIN_W)
    dc_idx = jnp.clip(col[None, :] - col[:, None] + NA_WIN_W - 1, 0, 2 * NA_WIN_W - 2)
    tab = rpb[:, :, dc_idx] * LOG2_E
    tab = jnp.where(col_mask[None, None], tab, MASK_VALUE)
    tab = jnp.concatenate([tab, jnp.full_like(tab[:, :1], MASK_VALUE)], axis=1)
    zeros = jnp.zeros_like(tab)
    left = jnp.concatenate([tab, zeros], axis=-1)
    right = jnp.concatenate([zeros, tab], axis=-1)
    return jnp.stack([left, right], axis=1).astype(F32)


def _na_attention(qkv, tab, need_ctx):
    b, t, _ = qkv.shape
    h = NA_HEADS
    kern = functools.partial(_na_kernel, need_ctx=need_ctx, n_lat=t - CTX_LEN)
    return pl.pallas_call(
        kern,
        grid=(b, h),
        in_specs=[
            pl.BlockSpec((1, t, HEAD_DIM), lambda bi, hi: (bi, 0, hi)),
            pl.BlockSpec((1, t, HEAD_DIM), lambda bi, hi: (bi, 0, h + hi)),
            pl.BlockSpec((1, t, HEAD_DIM), lambda bi, hi: (bi, 0, 2 * h + hi)),
            pl.BlockSpec((1, 2, 2 * NA_WIN_H, GRID_W, LANES), lambda bi, hi: (hi, 0, 0, 0, 0)),
        ],
        out_specs=pl.BlockSpec((1, t, HEAD_DIM), lambda bi, hi: (bi, 0, hi)),
        out_shape=jax.ShapeDtypeStruct((b, t, h * HEAD_DIM), BF16),
        scratch_shapes=[pltpu.VMEM((t, 2 * HEAD_DIM), BF16)],
        compiler_params=_cparams(2),
        name="na_attn",
    )(qkv, qkv, qkv, tab)


def _rope_tables(n_tok):
    t = jnp.arange(n_tok)
    row = (t // GRID_W).astype(F32)
    col = (t % GRID_W).astype(F32)
    half = HEAD_DIM // 2
    inv_freq = ROPE_THETA ** (-jnp.arange(0, half, 2, dtype=F32) / half)
    ang_r = row[:, None] * inv_freq[None, :]
    ang_c = col[:, None] * inv_freq[None, :]
    ang = jnp.concatenate([ang_r, ang_r, ang_c, ang_c], axis=-1)
    cos, sin = jnp.cos(ang), jnp.sin(ang)
    upper = (jnp.arange(HEAD_DIM) % half) >= (half // 2)
    sin_a = jnp.where(upper[None, :], sin, 0.0)
    sin_b = jnp.where(upper[None, :], 0.0, -sin)
    return cos, sin_a, sin_b


def _rope(x, cos, sin_a, sin_b):
    quarter = HEAD_DIM // 4
    return x * cos + pltpu.roll(x, quarter, 1) * sin_a + pltpu.roll(x, HEAD_DIM - quarter, 1) * sin_b


def _gqa_kernel(q_ref, k_ref, v_ref, cos_ref, sa_ref, sb_ref, qg_ref, kg_ref, o_ref, kn_ref, va_ref, *, need_ctx, n_lat):
    L = CTX_LEN
    G = GQA_HEADS // GQA_KV_HEADS
    scale = SCORE_SCALE
    tq = 128

    va_ref[...] = _ones_augmented(v_ref[0])
    kg = kg_ref[...]
    kn_ref[:L, :] = _rms(k_ref[0, :L, :].astype(F32), kg).astype(BF16)
    k_lat = _rms(k_ref[0, L:, :].astype(F32), kg)
    kn_ref[L:, :] = _rope(k_lat, cos_ref[...], sa_ref[...], sb_ref[...]).astype(BF16)

    qg = qg_ref[...] * scale

    if need_ctx:
        qs = [_rms(q_ref[0, :L, hh * HEAD_DIM:(hh + 1) * HEAD_DIM].astype(F32), qg).astype(BF16) for hh in range(G)]
        s = lax.dot_general(jnp.concatenate(qs, axis=0), kn_ref[:L, :], _NT, preferred_element_type=F32)
        p, l = _softmax_parts(s)
        o = jnp.dot(p.astype(BF16), v_ref[0, :L, :], preferred_element_type=F32) / l
        for hh in range(G):
            o_ref[0, :L, hh * HEAD_DIM:(hh + 1) * HEAD_DIM] = o[hh * L:(hh + 1) * L].astype(BF16)
    else:
        o_ref[0, :L, :] = jnp.zeros((L, G * HEAD_DIM), BF16)

    def body(mq, carry):
        r0 = pl.multiple_of(mq * tq, tq)
        q0 = pl.multiple_of(L + mq * tq, tq)
        cos = cos_ref[pl.ds(r0, tq), :]
        sa = sa_ref[pl.ds(r0, tq), :]
        sb = sb_ref[pl.ds(r0, tq), :]
        for hh in range(G):
            qn = _rms(q_ref[0, pl.ds(q0, tq), hh * HEAD_DIM:(hh + 1) * HEAD_DIM].astype(F32), qg)
            qh = _rope(qn, cos, sa, sb).astype(BF16)
            o = _attend_augmented(qh, kn_ref[...], va_ref[...])
            o_ref[0, pl.ds(q0, tq), hh * HEAD_DIM:(hh + 1) * HEAD_DIM] = o.astype(BF16)
        return carry

    lax.fori_loop(0, n_lat // tq, body, 0, unroll=8)


def _gqa_attention(qkv, rope, q_g, k_g, need_ctx):
    b, t, _ = qkv.shape
    n_lat = t - CTX_LEN
    G = GQA_HEADS // GQA_KV_HEADS
    gw = G * HEAD_DIM
    cos, sin_a, sin_b = rope
    kern = functools.partial(_gqa_kernel, need_ctx=need_ctx, n_lat=n_lat)
    tab_spec = pl.BlockSpec((n_lat, HEAD_DIM), lambda bi, gi: (0, 0))
    vec_spec = pl.BlockSpec((1, HEAD_DIM), lambda bi, gi: (0, 0))
    return pl.pallas_call(
        kern,
        grid=(b, GQA_KV_HEADS),
        in_specs=[
            pl.BlockSpec((1, t, gw), lambda bi, gi: (bi, 0, gi)),
            pl.BlockSpec((1, t, HEAD_DIM), lambda bi, gi: (bi, 0, GQA_HEADS + gi)),
            pl.BlockSpec((1, t, HEAD_DIM), lambda bi, gi: (bi, 0, GQA_HEADS + GQA_KV_HEADS + gi)),
            tab_spec, tab_spec, tab_spec, vec_spec, vec_spec,
        ],
        out_specs=pl.BlockSpec((1, t, gw), lambda bi, gi: (bi, 0, gi)),
        out_shape=jax.ShapeDtypeStruct((b, t, GQA_HEADS * HEAD_DIM), BF16),
        scratch_shapes=[pltpu.VMEM((t, HEAD_DIM), BF16), pltpu.VMEM((t, 2 * HEAD_DIM), BF16)],
        compiler_params=_cparams(2),
        name="gqa_attn",
    )(qkv, qkv, qkv, cos, sin_a, sin_b, q_g.reshape(1, HEAD_DIM), k_g.reshape(1, HEAD_DIM))


def _diff_kernel(q_ref, k_ref, v_ref, cos_ref, sa_ref, sb_ref, lam_ref, g_ref, o_ref, kr_ref, *, need_ctx, n_lat, lambda_init):
    L = CTX_LEN
    scale = SCORE_SCALE
    tq = 128
    lam = lam_ref[...]
    lam_full = (jnp.exp(jnp.sum(lam[0:1] * lam[1:2], axis=-1, keepdims=True))
                - jnp.exp(jnp.sum(lam[2:3] * lam[3:4], axis=-1, keepdims=True)) + lambda_init)
    g = g_ref[...] * (1.0 - lambda_init)

    kr_ref[:L, :] = k_ref[0, :L, :]
    for c in range(2):
        k_lat = k_ref[0, L:, c * HEAD_DIM:(c + 1) * HEAD_DIM].astype(F32)
        kr_ref[L:, c * HEAD_DIM:(c + 1) * HEAD_DIM] = _rope(k_lat, cos_ref[...], sa_ref[...], sb_ref[...]).astype(BF16)

    def attend(qs, n_keys):
        ps = []
        for c in range(2):
            s = lax.dot_general(qs[c], kr_ref[:n_keys, c * HEAD_DIM:(c + 1) * HEAD_DIM], _NT, preferred_element_type=F32)
            p, l = _softmax_parts(s)
            ps.append(p * ((1.0 if c == 0 else lam_full) / l))
        a = (ps[0] - ps[1]).astype(BF16)
        o = jnp.dot(a, v_ref[0, :n_keys, :], preferred_element_type=F32)
        return _rms(o, g).astype(BF16)

    if need_ctx:
        qs = [(q_ref[0, :L, c * HEAD_DIM:(c + 1) * HEAD_DIM].astype(F32) * scale).astype(BF16) for c in range(2)]
        o_ref[0, :L, :] = attend(qs, L)
    else:
        o_ref[0, :L, :] = jnp.zeros((L, 2 * HEAD_DIM), BF16)

    def body(mq, carry):
        r0 = pl.multiple_of(mq * tq, tq)
        q0 = pl.multiple_of(L + mq * tq, tq)
        cos = cos_ref[pl.ds(r0, tq), :]
        sa = sa_ref[pl.ds(r0, tq), :]
        sb = sb_ref[pl.ds(r0, tq), :]
        qs = []
        for c in range(2):
            q = q_ref[0, pl.ds(q0, tq), c * HEAD_DIM:(c + 1) * HEAD_DIM].astype(F32) * scale
            qs.append(_rope(q, cos, sa, sb).astype(BF16))
        o_ref[0, pl.ds(q0, tq), :] = attend(qs, L + n_lat)
        return carry

    lax.fori_loop(0, n_lat // tq, body, 0, unroll=8)


def _diff_attention(qkv, rope, lam, subln_g, layer_idx, need_ctx):
    b, t, _ = qkv.shape
    n_lat = t - CTX_LEN
    hw = 2 * HEAD_DIM
    cos, sin_a, sin_b = rope
    lambda_init = 0.8 - 0.6 * math.exp(-0.3 * layer_idx)
    kern = functools.partial(_diff_kernel, need_ctx=need_ctx, n_lat=n_lat, lambda_init=lambda_init)
    tab_spec = pl.BlockSpec((n_lat, HEAD_DIM), lambda bi, hi: (0, 0))
    return pl.pallas_call(
        kern,
        grid=(b, DIFF_HEADS),
        in_specs=[
            pl.BlockSpec((1, t, hw), lambda bi, hi: (bi, 0, hi)),
            pl.BlockSpec((1, t, hw), lambda bi, hi: (bi, 0, DIFF_HEADS + hi)),
            pl.BlockSpec((1, t, hw), lambda bi, hi: (bi, 0, 2 * DIFF_HEADS + hi)),
            tab_spec, tab_spec, tab_spec,
            pl.BlockSpec((4, HEAD_DIM), lambda bi, hi: (0, 0)),
            pl.BlockSpec((1, hw), lambda bi, hi: (0, 0)),
        ],
        out_specs=pl.BlockSpec((1, t, hw), lambda bi, hi: (bi, 0, hi)),
        out_shape=jax.ShapeDtypeStruct((b, t, DIFF_HEADS * hw), BF16),
        scratch_shapes=[pltpu.VMEM((t, hw), BF16)],
        compiler_params=_cparams(2),
        name="diff_attn",
    )(qkv, qkv, qkv, cos, sin_a, sin_b, lam, subln_g.reshape(1, hw))


def _router_kernel(x_ref, g_ref, sh_ref, sc_ref, rw_ref, rb_ref, h_ref, e_ref, r_ref, gt_ref, cnt_ref, run_ref, *, tpb, ctx_row):
    i = pl.program_id(0)

    @pl.when(i == 0)
    def _():
        run_ref[...] = jnp.zeros_like(run_ref)

    y = _rms(x_ref[...], g_ref[...])
    h = y * (1.0 + _row_mod(i, tpb, ctx_row, sc_ref)) + _row_mod(i, tpb, ctx_row, sh_ref)
    h_ref[...] = _pack_bf16_pairs(h)

    h_hi = h.astype(BF16)
    h_lo = (h - h_hi.astype(F32)).astype(BF16)
    w = rw_ref[...]
    w_hi = w.astype(BF16)
    w_lo = (w - w_hi.astype(F32)).astype(BF16)
    logits = (jnp.dot(h_hi, w_hi, preferred_element_type=F32) + jnp.dot(h_lo, w_hi, preferred_element_type=F32)
              + jnp.dot(h_hi, w_lo, preferred_element_type=F32)) + rb_ref[...]

    lane = lax.broadcasted_iota(jnp.int32, (ROW_TILE, LANES), 1)
    lane_f = lane.astype(F32)
    work = jnp.where(lane < N_EXPERTS, logits, -jnp.inf)
    vals, onehots = [], []
    for _ in range(TOP_K):
        m = jnp.max(work, axis=-1, keepdims=True)
        idx = jnp.min(jnp.where(work == m, lane_f, float(LANES)), axis=-1, keepdims=True)
        hit = lane_f == idx
        vals.append(m)
        onehots.append(hit)
        work = jnp.where(hit, -jnp.inf, work)

    exps = [jnp.exp(v - vals[0]) for v in vals]
    denom = exps[0] + exps[1] + exps[2] + exps[3]

    member = onehots[0]
    for k in range(1, TOP_K):
        member = jnp.logical_or(member, onehots[k])
    member_f = member.astype(F32)
    r_i = lax.broadcasted_iota(jnp.int32, (ROW_TILE, ROW_TILE), 0)
    c_i = lax.broadcasted_iota(jnp.int32, (ROW_TILE, ROW_TILE), 1)
    strict_lower = (c_i < r_i).astype(BF16)
    before = jnp.dot(strict_lower, member_f.astype(BF16), preferred_element_type=F32) + run_ref[...]

    e_out = jnp.zeros((ROW_TILE, LANES), F32)
    r_out = jnp.zeros((ROW_TILE, LANES), F32)
    g_out = jnp.zeros((ROW_TILE, LANES), F32)
    for k in range(TOP_K):
        hit = onehots[k]
        e_k = jnp.sum(jnp.where(hit, lane_f, 0.0), axis=-1, keepdims=True)
        r_k = jnp.sum(jnp.where(hit, before, 0.0), axis=-1, keepdims=True)
        e_out = jnp.where(lane == k, e_k, e_out)
        r_out = jnp.where(lane == k, r_k, r_out)
        g_out = jnp.where(lane == k, exps[k] / denom, g_out)
    e_ref[...] = e_out.astype(jnp.int32)
    r_ref[...] = r_out.astype(jnp.int32)
    gt_ref[...] = g_out

    run_ref[...] = run_ref[...] + jnp.sum(member_f, axis=0, keepdims=True)
    cnt_ref[...] = run_ref[...].astype(jnp.int32)


def _router(xs, g, mod, router_w, router_b, tpb, ctx_row):
    n_tok, d = xs.shape
    rw = jnp.zeros((d, LANES), F32).at[:, :N_EXPERTS].set(router_w)
    rb = jnp.zeros((1, LANES), F32).at[0, :N_EXPERTS].set(router_b)
    kern = functools.partial(_router_kernel, tpb=tpb, ctx_row=ctx_row)
    row_spec = pl.BlockSpec((ROW_TILE, LANES), lambda i: (i, 0))
    return pl.pallas_call(
        kern,
        grid=(n_tok // ROW_TILE,),
        in_specs=[
            pl.BlockSpec((ROW_TILE, d), lambda i: (i, 0)),
            pl.BlockSpec((1, d), lambda i: (0, 0)),
            pl.BlockSpec((MOD_ROWS, d), lambda i: (0, 3)),
            pl.BlockSpec((MOD_ROWS, d), lambda i: (0, 4)),
            pl.BlockSpec((d, LANES), lambda i: (0, 0)),
            pl.BlockSpec((1, LANES), lambda i: (0, 0)),
        ],
        out_specs=[
            pl.BlockSpec((ROW_TILE, d // 2), lambda i: (i, 0)),
            row_spec, row_spec, row_spec,
            pl.BlockSpec((1, LANES), lambda i: (0, 0)),
        ],
        out_shape=[
            jax.ShapeDtypeStruct((n_tok, d // 2), jnp.int32),
            jax.ShapeDtypeStruct((n_tok, LANES), jnp.int32),
            jax.ShapeDtypeStruct((n_tok, LANES), jnp.int32),
            jax.ShapeDtypeStruct((n_tok, LANES), F32),
            jax.ShapeDtypeStruct((1, LANES), jnp.int32),
        ],
        scratch_shapes=[pltpu.VMEM((1, LANES), F32)],
        compiler_params=_cparams(1),
        name="moe_router",
    )(xs, g.reshape(1, d), mod, mod, rw, rb)


DISPATCH_TILE = 512


def _dispatch_kernel(zero_blocks_ref, dest_ref, h_ref, slots_ref, zero_ref, sem, zero_sem):
    @pl.when(pl.program_id(0) == 0)
    def _():
        zero_ref[...] = jnp.zeros_like(zero_ref)

        def zero_copy(j):
            rows = pl.ds(pl.multiple_of(zero_blocks_ref[j] * EXPERT_TILE, EXPERT_TILE), EXPERT_TILE)
            return pltpu.make_async_copy(zero_ref, slots_ref.at[rows, :], zero_sem)

        def for_each_zero_block(fn):
            def body(j, carry):
                @pl.when(zero_blocks_ref[j] >= 0)
                def _():
                    fn(j)
                return carry
            lax.fori_loop(0, zero_blocks_ref.shape[0], body, 0)

        for_each_zero_block(lambda j: zero_copy(j).start())
        for_each_zero_block(lambda j: zero_copy(j).wait())

    def row_copy(t, k):
        d = dest_ref[t * TOP_K + k]
        return pltpu.make_async_copy(h_ref.at[pl.ds(t, 1), :], slots_ref.at[pl.ds(d, 1), :], sem)

    def start(t, carry):
        for k in range(TOP_K):
            row_copy(t, k).start(priority=k % 2)
        return carry

    def wait(t, carry):
        for k in range(TOP_K):
            row_copy(t, k).wait()
        return carry

    lax.fori_loop(0, DISPATCH_TILE, start, 0)
    lax.fori_loop(0, DISPATCH_TILE, wait, 0)


def _dispatch(h_packed, dest_flat, zero_blocks, n_slots):
    n_tok, w = h_packed.shape
    grid_spec = pltpu.PrefetchScalarGridSpec(
        num_scalar_prefetch=1,
        grid=(n_tok // DISPATCH_TILE,),
        in_specs=[
            pl.BlockSpec((DISPATCH_TILE * TOP_K,), lambda i, zb: (i,), memory_space=pltpu.SMEM),
            pl.BlockSpec((DISPATCH_TILE, w), lambda i, zb: (i, 0)),
        ],
        out_specs=pl.BlockSpec(memory_space=pl.ANY),
        scratch_shapes=[pltpu.VMEM((EXPERT_TILE, w), jnp.int32), pltpu.SemaphoreType.DMA(()), pltpu.SemaphoreType.DMA(())],
    )
    return pl.pallas_call(
        _dispatch_kernel,
        grid_spec=grid_spec,
        out_shape=jax.ShapeDtypeStruct((n_slots, w), jnp.int32),
        compiler_params=_cparams(1),
        name="moe_dispatch",
    )(zero_blocks, dest_flat, h_packed)


CAST_ROWS = 256


def _expert_kernel(be_ref, nx_ref, nu_ref, x_ref, bg_ref, bu_ref, bd_ref, wg_hbm, wu_hbm, wd_hbm, y_ref,
                   stage_g, stage_u, stage_d, wg_bf, wu_bf, wd_bf, sem, *, layer):
    b = pl.program_id(0)
    used = b < nu_ref[0]
    e = be_ref[b]
    first = jnp.logical_or(b == 0, be_ref[jnp.maximum(b - 1, 0)] != e)

    def weight_copies(ex):
        return (pltpu.make_async_copy(wg_hbm.at[layer, ex], stage_g, sem.at[0]),
                pltpu.make_async_copy(wu_hbm.at[layer, ex], stage_u, sem.at[1]),
                pltpu.make_async_copy(wd_hbm.at[layer, ex], stage_d, sem.at[2]))

    @pl.when(b == 0)
    def _():
        for cp in weight_copies(e):
            cp.start()

    @pl.when(jnp.logical_and(used, first))
    def _():
        for cp in weight_copies(e):
            cp.wait()

        def cast(src, dst):
            def step(r, carry):
                rows = pl.ds(pl.multiple_of(r * CAST_ROWS, CAST_ROWS), CAST_ROWS)
                dst[rows, :] = src[rows, :].astype(BF16)
                return carry
            lax.fori_loop(0, src.shape[0] // CAST_ROWS, step, 0)

        cast(stage_g, wg_bf)
        cast(stage_u, wu_bf)
        cast(stage_d, wd_bf)

        @pl.when(nx_ref[b] >= 0)
        def _():
            for cp in weight_copies(nx_ref[b]):
                cp.start()

    @pl.when(used)
    def _():
        x = _unpack_bf16_pairs(x_ref[...]).astype(BF16)
        g = jnp.minimum(jnp.dot(x, wg_bf[...], preferred_element_type=F32) + bg_ref[0, 0], SWIGLU_LIMIT)
        u = jnp.clip(jnp.dot(x, wu_bf[...], preferred_element_type=F32) + bu_ref[0, 0], -SWIGLU_LIMIT, SWIGLU_LIMIT)
        act = (u + 1.0) * (g * jax.nn.sigmoid(SWIGLU_ALPHA * g))
        y_ref[...] = _pack_bf16_pairs(jnp.dot(act.astype(BF16), wd_bf[...], preferred_element_type=F32) + bd_ref[0, 0])

    @pl.when(jnp.logical_not(used))
    def _():
        y_ref[...] = jnp.zeros_like(y_ref)


def _experts(x_slots, block_expert, next_expert, n_used, layer, wg, bg, wu, bu, wd, bd):
    n_slots, w = x_slots.shape
    n_blocks = n_slots // EXPERT_TILE
    depth, n_exp, d, de = wg.shape

    def used_block(b, be, nx, nu):
        return (jnp.minimum(b, nu[0] - 1), 0)

    def bias_spec(width):
        return pl.BlockSpec((1, 1, 1, width), lambda b, be, nx, nu: (layer, be[b], 0, 0))

    hbm = pl.BlockSpec(memory_space=pl.ANY)
    grid_spec = pltpu.PrefetchScalarGridSpec(
        num_scalar_prefetch=3,
        grid=(n_blocks,),
        in_specs=[pl.BlockSpec((EXPERT_TILE, w), used_block), bias_spec(de), bias_spec(de), bias_spec(d), hbm, hbm, hbm],
        out_specs=pl.BlockSpec((EXPERT_TILE, w), lambda b, be, nx, nu: (b, 0)),
        scratch_shapes=[
            pltpu.VMEM((d, de), F32), pltpu.VMEM((d, de), F32), pltpu.VMEM((de, d), F32),
            pltpu.VMEM((d, de), BF16), pltpu.VMEM((d, de), BF16), pltpu.VMEM((de, d), BF16),
            pltpu.SemaphoreType.DMA((3,)),
        ],
    )
    return pl.pallas_call(
        functools.partial(_expert_kernel, layer=layer),
        grid_spec=grid_spec,
        out_shape=jax.ShapeDtypeStruct((n_slots, w), jnp.int32),
        compiler_params=_cparams(1),
        name="moe_experts",
    )(block_expert, next_expert, n_used, x_slots,
      bg.reshape(depth, n_exp, 1, de), bu.reshape(depth, n_exp, 1, de), bd.reshape(depth, n_exp, 1, d), wg, wu, wd)


COMBINE_TILE = 256


def _combine_kernel(dest_ref, dest_next_ref, y_hbm, gt_ref, x_ref, gate_ref, o_ref, buf, sem, *, tiles_per_batch, ctx_tiles, ctx_row):
    i = pl.program_id(0)
    n = pl.num_programs(0)
    slot = i % 2

    def row_copy(dref, s, t, k):
        return pltpu.make_async_copy(y_hbm.at[pl.ds(dref[t * TOP_K + k], 1), :], buf.at[s, k, pl.ds(t, 1), :], sem.at[s])

    def start_tile(dref, s):
        def body(t, carry):
            for k in range(TOP_K):
                row_copy(dref, s, t, k).start(priority=k % 2)
            return carry
        lax.fori_loop(0, COMBINE_TILE, body, 0)

    @pl.when(i == 0)
    def _():
        start_tile(dest_ref, 0)

    @pl.when(i + 1 < n)
    def _():
        start_tile(dest_next_ref, 1 - slot)

    def wait_body(t, carry):
        for k in range(TOP_K):
            row_copy(dest_ref, slot, t, k).wait()
        return carry
    lax.fori_loop(0, COMBINE_TILE, wait_body, 0)

    gates = gt_ref[...]
    y = gates[:, 0:1] * _unpack_bf16_pairs(buf[slot, 0])
    for k in range(1, TOP_K):
        y = y + gates[:, k:k + 1] * _unpack_bf16_pairs(buf[slot, k])
    b = i // tiles_per_batch
    is_ctx = (i % tiles_per_batch) < ctx_tiles
    mod_gate = jnp.where(is_ctx, gate_ref[ctx_row:ctx_row + 1, :], gate_ref[pl.ds(b, 1), :])
    o_ref[...] = x_ref[...] + mod_gate * y


def _combine(y_slots, dest_flat, gates, xs, mod, rows_per_batch, ctx_row):
    n_tok, d = xs.shape
    w = y_slots.shape[1]
    n_tiles = n_tok // COMBINE_TILE
    kern = functools.partial(_combine_kernel, tiles_per_batch=rows_per_batch // COMBINE_TILE,
                             ctx_tiles=CTX_LEN // COMBINE_TILE, ctx_row=ctx_row)
    return pl.pallas_call(
        kern,
        grid=(n_tiles,),
        in_specs=[
            pl.BlockSpec((COMBINE_TILE * TOP_K,), lambda i: (i,), memory_space=pltpu.SMEM),
            pl.BlockSpec((COMBINE_TILE * TOP_K,), lambda i: (jnp.minimum(i + 1, n_tiles - 1),), memory_space=pltpu.SMEM),
            pl.BlockSpec(memory_space=pl.ANY),
            pl.BlockSpec((COMBINE_TILE, LANES), lambda i: (i, 0)),
            pl.BlockSpec((COMBINE_TILE, d), lambda i: (i, 0)),
            pl.BlockSpec((MOD_ROWS, d), lambda i: (0, 5)),
        ],
        out_specs=pl.BlockSpec((COMBINE_TILE, d), lambda i: (i, 0)),
        out_shape=jax.ShapeDtypeStruct((n_tok, d), F32),
        scratch_shapes=[pltpu.VMEM((2, TOP_K, COMBINE_TILE, w), jnp.int32), pltpu.SemaphoreType.DMA((2,))],
        compiler_params=_cparams(1),
        name="moe_combine",
    )(dest_flat, dest_flat, y_slots, gates, xs, mod)


def _final_norm_kernel(x_ref, g_ref, o_ref):
    o_ref[0] = _rms(x_ref[0], g_ref[...])


def _final_norm(xs3, g):
    b, t, d = xs3.shape
    blk = CTX_LEN
    n_lat = t - CTX_LEN
    return pl.pallas_call(
        _final_norm_kernel,
        grid=(b, n_lat // blk),
        in_specs=[
            pl.BlockSpec((1, blk, d), lambda bi, j: (bi, j + 1, 0)),
            pl.BlockSpec((1, d), lambda bi, j: (0, 0)),
        ],
        out_specs=pl.BlockSpec((1, blk, d), lambda bi, j: (bi, j, 0)),
        out_shape=jax.ShapeDtypeStruct((b, n_lat, d), F32),
        compiler_params=_cparams(2),
        name="final_norm",
    )(xs3, g.reshape(1, d))


def _moe_layer(xs, g, mod, rows_per_batch, ctx_row, layer, router_w, router_b, wg, bg, wu, bu, wd, bd):
    n_tok, d = xs.shape
    tpb = rows_per_batch // ROW_TILE
    h, e_pad, r_pad, gates, cnt = _router(xs, g, mod, router_w, router_b, tpb, ctx_row)
    counts = cnt[0, :N_EXPERTS]
    padded = ((counts + EXPERT_TILE - 1) // EXPERT_TILE) * EXPERT_TILE
    pend = jnp.cumsum(padded)
    pstart = pend - padded
    n_blocks = (n_tok * TOP_K) // EXPERT_TILE + N_EXPERTS
    experts = jnp.arange(N_EXPERTS, dtype=jnp.int32)
    e_idx = e_pad[:, :TOP_K]
    dest = (jnp.sum(jnp.where(e_idx[..., None] == experts, pstart, 0), axis=-1) + r_pad[:, :TOP_K]).reshape(-1).astype(jnp.int32)
    block_row0 = jnp.arange(n_blocks, dtype=jnp.int32) * EXPERT_TILE
    block_expert = jnp.minimum(jnp.sum(pend[None, :] <= block_row0[:, None], axis=1), N_EXPERTS - 1).astype(jnp.int32)
    n_used = (pend[-1:] // EXPERT_TILE).astype(jnp.int32)
    later = jnp.logical_and(block_expert[None, :] > block_expert[:, None], (block_row0 < pend[-1])[None, :])
    next_expert = jnp.min(jnp.where(later, block_expert[None, :], N_EXPERTS), axis=1)
    next_expert = jnp.where(next_expert == N_EXPERTS, -1, next_expert).astype(jnp.int32)
    last_block = jnp.where(padded > counts, pend // EXPERT_TILE - 1, -1)
    tail_block = n_used[0] + experts
    zero_blocks = jnp.concatenate([last_block, jnp.where(tail_block < n_blocks, tail_block, -1)]).astype(jnp.int32)
    x_slots = _dispatch(h, dest, zero_blocks, n_blocks * EXPERT_TILE)
    y_slots = _experts(x_slots, block_expert, next_expert, n_used, layer, wg, bg, wu, bu, wd, bd)
    return _combine(y_slots, dest, gates, xs, mod, rows_per_batch, ctx_row)


def kernel(x, c, ctx, c_ctx, ada_w, ada_b, norm_mix_g, norm_ffn_g, final_g, na_wqkv, na_wo, na_rpb, gqa_wqkv, gqa_wo, gqa_q_g, gqa_k_g, diff_wqkv, diff_wo, diff_lambda, diff_subln_g, router_w, router_b, w_gate, b_gate, w_up, b_up, w_down, b_down):
    B, S, D = x.shape
    L = ctx.shape[1]
    T = L + S
    assert D == D_MODEL and L == CTX_LEN and T % ROW_TILE == 0 and B < MOD_ROWS
    assert (B * T) % DISPATCH_TILE == 0 and T % COMBINE_TILE == 0 and L % COMBINE_TILE == 0
    tpb = T // ROW_TILE
    ctx_row = B

    cvec = jnp.zeros((MOD_ROWS, D), F32).at[:B].set(c).at[B].set(c_ctx)
    mods = _ada_mod(cvec, ada_w, ada_b)
    rope = _rope_tables(S)

    xs = jnp.concatenate([ctx, x], axis=1).reshape(B * T, D)
    for i in range(DEPTH):
        need_ctx = i < DEPTH - 1
        mod = mods[i]
        kind, j = i % 3, i // 3
        if kind == 0:
            q_cols = jnp.arange(3 * D) < D
            w_qkv = (na_wqkv[j] * jnp.where(q_cols, SCORE_SCALE, 1.0)).astype(BF16)
            qkv = _norm_mm(xs, norm_mix_g[i], mod, w_qkv, tpb, ctx_row).reshape(B, T, -1)
            o = _na_attention(qkv, _na_bias_table(na_rpb[j]), need_ctx)
            w_o = na_wo[j]
        elif kind == 1:
            qkv = _norm_mm(xs, norm_mix_g[i], mod, gqa_wqkv[j].astype(BF16), tpb, ctx_row).reshape(B, T, -1)
            o = _gqa_attention(qkv, rope, gqa_q_g[j], gqa_k_g[j], need_ctx)
            w_o = gqa_wo[j]
        else:
            qkv = _norm_mm(xs, norm_mix_g[i], mod, diff_wqkv[j].astype(BF16), tpb, ctx_row).reshape(B, T, -1)
            o = _diff_attention(qkv, rope, diff_lambda[j], diff_subln_g[j], i, need_ctx)
            w_o = diff_wo[j]
        xs = _out_proj(o.reshape(B * T, D), w_o.astype(BF16), xs, mod, tpb, ctx_row)
        xs = _moe_layer(xs, norm_ffn_g[i], mod, T, ctx_row, i, router_w[i], router_b[i],
                        w_gate, b_gate, w_up, b_up, w_down, b_down)
    return _final_norm(xs.reshape(B, T, D), final_g)
```

```python
import functools
import math

import jax
import jax.numpy as jnp
from jax import lax
from jax.experimental import pallas as pl
from jax.experimental.pallas import tpu as pltpu

F32 = jnp.float32
BF16 = jnp.bfloat16

D_MODEL = 2048
DEPTH = 4
GRID_W = 64
CTX_LEN = 256
HEAD_DIM = 128
ROPE_THETA = 10000.0
NORM_EPS = 1e-6
MASK_VALUE = -1e30
NA_HEADS = 16
NA_WIN_H = 8
NA_WIN_W = 16
GQA_HEADS = 16
GQA_KV_HEADS = 4
DIFF_HEADS = 8
N_EXPERTS = 32
TOP_K = 4
D_EXPERT = 768
SWIGLU_LIMIT = 7.0
SWIGLU_ALPHA = 1.702

LANES = 128
ROW_TILE = 768
EXPERT_TILE = 256
MOD_ROWS = 8
VMEM_LIMIT = 56 * 1024 * 1024

_NT = (((1,), (1,)), ((), ()))
LOG2_E = math.log2(math.e)
SCORE_SCALE = HEAD_DIM ** -0.5 * LOG2_E


def _cparams(n_axes, vmem=VMEM_LIMIT):
    return pltpu.CompilerParams(dimension_semantics=("arbitrary",) * n_axes, vmem_limit_bytes=vmem)


def _row_mod(tile, tiles_per_batch, ctx_row, ref):
    b = tile // tiles_per_batch
    rows = lax.broadcasted_iota(jnp.int32, (ROW_TILE, 1), 0)
    is_ctx = jnp.logical_and(tile % tiles_per_batch == 0, rows < CTX_LEN)
    return jnp.where(is_ctx, ref[ctx_row:ctx_row + 1, :], ref[pl.ds(b, 1), :])


def _rms(x, g):
    return x * lax.rsqrt(jnp.mean(x * x, axis=-1, keepdims=True) + NORM_EPS) * g


def _pack_bf16_pairs(x):
    w = x.shape[1] // 2
    u = lax.bitcast_convert_type(x, jnp.int32)
    r = u + 0x7FFF + (lax.shift_right_logical(u, 16) & 1)
    return (r[:, w:] & -65536) | lax.shift_right_logical(r[:, :w], 16)


def _unpack_bf16_pairs(p):
    lo = lax.bitcast_convert_type(lax.shift_left(p, 16), F32)
    hi = lax.bitcast_convert_type(p & -65536, F32)
    return jnp.concatenate([lo, hi], axis=1)


def _ones_augmented(v):
    return jnp.concatenate([v, jnp.ones_like(v)], axis=1)


def _attend_augmented(q, k, v_aug):
    s = lax.dot_general(q, k, _NT, preferred_element_type=F32)
    p = jnp.exp2((s - jnp.max(s, axis=-1, keepdims=True)).astype(BF16))
    o = jnp.dot(p, v_aug, preferred_element_type=F32)
    d = o.shape[1] // 2
    return o[:, :d] / o[:, d:d + 1]


SLOT_ROW = (D_MODEL // 2 // LANES, LANES)


def _to_row_tiles(p):
    return p.reshape(p.shape[0], *SLOT_ROW)


def _from_row_tiles(p):
    return p.reshape(p.shape[0], SLOT_ROW[0] * SLOT_ROW[1])


def _softmax_parts(s):
    m = jnp.max(s, axis=-1, keepdims=True)
    p = jnp.exp2(s - m)
    return p, jnp.sum(p, axis=-1, keepdims=True)


def _ada_kernel(c_ref, w_ref, b_ref, o_ref):
    c = c_ref[...]
    s = c * jax.nn.sigmoid(c)
    o_ref[0] = jnp.dot(s.astype(BF16), w_ref[0].astype(BF16), preferred_element_type=F32) + b_ref[0]


def _ada_mod(cvec, ada_w, ada_b):
    depth, d, n = ada_w.shape
    tn = 1024
    return pl.pallas_call(
        _ada_kernel,
        grid=(depth, n // tn),
        in_specs=[
            pl.BlockSpec((MOD_ROWS, d), lambda l, j: (0, 0)),
            pl.BlockSpec((1, d, tn), lambda l, j: (l, 0, j)),
            pl.BlockSpec((1, 1, tn), lambda l, j: (l, 0, j)),
        ],
        out_specs=pl.BlockSpec((1, MOD_ROWS, tn), lambda l, j: (l, 0, j)),
        out_shape=jax.ShapeDtypeStruct((depth, MOD_ROWS, n), F32),
        compiler_params=_cparams(2),
        name="ada_mod",
    )(cvec, ada_w, ada_b.reshape(depth, 1, n))


def _norm_mm_kernel(x_ref, g_ref, sh_ref, sc_ref, w_ref, o_ref, h_ref, *, tpb, ctx_row):
    i = pl.program_id(0)

    @pl.when(pl.program_id(1) == 0)
    def _():
        y = _rms(x_ref[...], g_ref[...])
        sh = _row_mod(i, tpb, ctx_row, sh_ref)
        sc = _row_mod(i, tpb, ctx_row, sc_ref)
        h_ref[...] = (y * (1.0 + sc) + sh).astype(BF16)

    o_ref[...] = jnp.dot(h_ref[...], w_ref[...], preferred_element_type=F32).astype(BF16)


def _norm_mm(xs, g, mod, w_bf16, tpb, ctx_row):
    n_tok, d = xs.shape
    n_out = w_bf16.shape[1]
    tn = 1024
    kern = functools.partial(_norm_mm_kernel, tpb=tpb, ctx_row=ctx_row)
    return pl.pallas_call(
        kern,
        grid=(n_tok // ROW_TILE, n_out // tn),
        in_specs=[
            pl.BlockSpec((ROW_TILE, d), lambda i, j: (i, 0)),
            pl.BlockSpec((1, d), lambda i, j: (0, 0)),
            pl.BlockSpec((MOD_ROWS, d), lambda i, j: (0, 0)),
            pl.BlockSpec((MOD_ROWS, d), lambda i, j: (0, 1)),
            pl.BlockSpec((d, tn), lambda i, j: (0, j)),
        ],
        out_specs=pl.BlockSpec((ROW_TILE, tn), lambda i, j: (i, j)),
        out_shape=jax.ShapeDtypeStruct((n_tok, n_out), BF16),
        scratch_shapes=[pltpu.VMEM((ROW_TILE, d), BF16)],
        compiler_params=_cparams(2),
        name="norm_qkv",
    )(xs, g.reshape(1, d), mod, mod, w_bf16)


def _out_proj_kernel(o_ref, w_ref, x_ref, gate_ref, y_ref, *, tpb, ctx_row):
    i = pl.program_id(0)
    gate = _row_mod(i, tpb, ctx_row, gate_ref)
    y = jnp.dot(o_ref[...], w_ref[...], preferred_element_type=F32)
    y_ref[...] = x_ref[...] + gate * y


def _out_proj(o, w_bf16, xs, mod, tpb, ctx_row):
    n_tok, d = xs.shape
    tn = 1024
    kern = functools.partial(_out_proj_kernel, tpb=tpb, ctx_row=ctx_row)
    return pl.pallas_call(
        kern,
        grid=(n_tok // ROW_TILE, d // tn),
        in_specs=[
            pl.BlockSpec((ROW_TILE, d), lambda i, j: (i, 0)),
            pl.BlockSpec((d, tn), lambda i, j: (0, j)),
            pl.BlockSpec((ROW_TILE, tn), lambda i, j: (i, j)),
            pl.BlockSpec((MOD_ROWS, tn), lambda i, j: (0, 2 * (D_MODEL // tn) + j)),
        ],
        out_specs=pl.BlockSpec((ROW_TILE, tn), lambda i, j: (i, j)),
        out_shape=jax.ShapeDtypeStruct((n_tok, d), F32),
        compiler_params=_cparams(2),
        name="out_proj",
    )(o, w_bf16, xs, mod)


def _na_kernel(q_ref, k_ref, v_ref, tab_ref, o_ref, va_ref, *, need_ctx, n_lat):
    L = CTX_LEN
    va_ref[...] = _ones_augmented(v_ref[0])
    kc = k_ref[0, :L, :]
    vc = va_ref[:L, :]
    if need_ctx:
        o_ref[0, :L, :] = _attend_augmented(q_ref[0, :L, :], kc, vc).astype(BF16)
    else:
        o_ref[0, :L, :] = jnp.zeros((L, HEAD_DIM), BF16)

    rows = n_lat // GRID_W
    qb_rows = 2
    win_rows = 10
    n_blocks = rows // qb_rows

    def body(mq, carry):
        q0 = pl.multiple_of(L + mq * (qb_rows * GRID_W), LANES)
        q = q_ref[0, pl.ds(q0, qb_rows * GRID_W), :]
        u0 = jnp.clip(qb_rows * mq - NA_WIN_H // 2, 0, rows - win_rows)
        k0 = pl.multiple_of(L + u0 * GRID_W, LANES)
        kw = k_ref[0, pl.ds(k0, win_rows * GRID_W), :]
        vw = va_ref[pl.ds(k0, win_rows * GRID_W), :]
        s_c = lax.dot_general(q, kc, _NT, preferred_element_type=F32)
        s_l = lax.dot_general(q, kw, _NT, preferred_element_type=F32)
        bias_rows = []
        for a in range(qb_rows):
            qr = qb_rows * mq + a
            r0q = jnp.clip(qr - NA_WIN_H // 2, 0, rows - NA_WIN_H)
            tiles = []
            for p in range(win_rows // 2):
                kr0 = u0 + 2 * p
                kr1 = kr0 + 1
                i0 = jnp.where(jnp.logical_and(kr0 >= r0q, kr0 < r0q + NA_WIN_H), kr0 - qr + NA_WIN_H - 1, 2 * NA_WIN_H - 1)
                i1 = jnp.where(jnp.logical_and(kr1 >= r0q, kr1 < r0q + NA_WIN_H), kr1 - qr + NA_WIN_H - 1, 2 * NA_WIN_H - 1)
                tiles.append(tab_ref[0, 0, i0] + tab_ref[0, 1, i1])
            bias_rows.append(jnp.concatenate(tiles, axis=1))
        s_l = s_l + jnp.concatenate(bias_rows, axis=0)
        m = jnp.maximum(jnp.max(s_c, axis=-1, keepdims=True), jnp.max(s_l, axis=-1, keepdims=True))
        p_c = jnp.exp2((s_c - m).astype(BF16))
        p_l = jnp.exp2((s_l - m).astype(BF16))
        o = jnp.dot(p_c, vc, preferred_element_type=F32) + jnp.dot(p_l, vw, preferred_element_type=F32)
        o_ref[0, pl.ds(q0, qb_rows * GRID_W), :] = (o[:, :HEAD_DIM] / o[:, HEAD_DIM:HEAD_DIM + 1]).astype(BF16)
        return carry

    lax.fori_loop(0, n_blocks, body, 0, unroll=8)


def _na_bias_table(rpb):
    col = jnp.arange(GRID_W)
    col_start = jnp.clip(col - NA_WIN_W // 2, 0, GRID_W - NA_WIN_W)
    col_mask = (col[None, :] >= col_start[:, None]) & (col[None, :] < col_start[:, None] + NA_WIN_W)
    dc_idx = jnp.clip(col[None, :] - col[:, None] + NA_WIN_W - 1, 0, 2 * NA_WIN_W - 2)
    tab = rpb[:, :, dc_idx] * LOG2_E
    tab = jnp.where(col_mask[None, None], tab, MASK_VALUE)
    tab = jnp.concatenate([tab, jnp.full_like(tab[:, :1], MASK_VALUE)], axis=1)
    zeros = jnp.zeros_like(tab)
    left = jnp.concatenate([tab, zeros], axis=-1)
    right = jnp.concatenate([zeros, tab], axis=-1)
    return jnp.stack([left, right], axis=1).astype(F32)


def _na_attention(qkv, tab, need_ctx):
    b, t, _ = qkv.shape
    h = NA_HEADS
    kern = functools.partial(_na_kernel, need_ctx=need_ctx, n_lat=t - CTX_LEN)
    return pl.pallas_call(
        kern,
        grid=(b, h),
        in_specs=[
            pl.BlockSpec((1, t, HEAD_DIM), lambda bi, hi: (bi, 0, hi)),
            pl.BlockSpec((1, t, HEAD_DIM), lambda bi, hi: (bi, 0, h + hi)),
            pl.BlockSpec((1, t, HEAD_DIM), lambda bi, hi: (bi, 0, 2 * h + hi)),
            pl.BlockSpec((1, 2, 2 * NA_WIN_H, GRID_W, LANES), lambda bi, hi: (hi, 0, 0, 0, 0)),
        ],
        out_specs=pl.BlockSpec((1, t, HEAD_DIM), lambda bi, hi: (bi, 0, hi)),
        out_shape=jax.ShapeDtypeStruct((b, t, h * HEAD_DIM), BF16),
        scratch_shapes=[pltpu.VMEM((t, 2 * HEAD_DIM), BF16)],
        compiler_params=_cparams(2),
        name="na_attn",
    )(qkv, qkv, qkv, tab)


def _rope_tables(n_tok):
    t = jnp.arange(n_tok)
    row = (t // GRID_W).astype(F32)
    col = (t % GRID_W).astype(F32)
    half = HEAD_DIM // 2
    inv_freq = ROPE_THETA ** (-jnp.arange(0, half, 2, dtype=F32) / half)
    ang_r = row[:, None] * inv_freq[None, :]
    ang_c = col[:, None] * inv_freq[None, :]
    ang = jnp.concatenate([ang_r, ang_r, ang_c, ang_c], axis=-1)
    cos, sin = jnp.cos(ang), jnp.sin(ang)
    upper = (jnp.arange(HEAD_DIM) % half) >= (half // 2)
    sin_a = jnp.where(upper[None, :], sin, 0.0)
    sin_b = jnp.where(upper[None, :], 0.0, -sin)
    return cos, sin_a, sin_b


def _rope(x, cos, sin_a, sin_b):
    quarter = HEAD_DIM // 4
    return x * cos + pltpu.roll(x, quarter, 1) * sin_a + pltpu.roll(x, HEAD_DIM - quarter, 1) * sin_b


def _gqa_kernel(q_ref, k_ref, v_ref, cos_ref, sa_ref, sb_ref, qg_ref, kg_ref, o_ref, kn_ref, va_ref, *, need_ctx, n_lat):
    L = CTX_LEN
    G = GQA_HEADS // GQA_KV_HEADS
    scale = SCORE_SCALE
    tq = 128

    va_ref[...] = _ones_augmented(v_ref[0])
    kg = kg_ref[...]
    kn_ref[:L, :] = _rms(k_ref[0, :L, :].astype(F32), kg).astype(BF16)
    k_lat = _rms(k_ref[0, L:, :].astype(F32), kg)
    kn_ref[L:, :] = _rope(k_lat, cos_ref[...], sa_ref[...], sb_ref[...]).astype(BF16)

    qg = qg_ref[...] * scale

    if need_ctx:
        qs = [_rms(q_ref[0, :L, hh * HEAD_DIM:(hh + 1) * HEAD_DIM].astype(F32), qg).astype(BF16) for hh in range(G)]
        s = lax.dot_general(jnp.concatenate(qs, axis=0), kn_ref[:L, :], _NT, preferred_element_type=F32)
        p, l = _softmax_parts(s)
        o = jnp.dot(p.astype(BF16), v_ref[0, :L, :], preferred_element_type=F32) / l
        for hh in range(G):
            o_ref[0, :L, hh * HEAD_DIM:(hh + 1) * HEAD_DIM] = o[hh * L:(hh + 1) * L].astype(BF16)
    else:
        o_ref[0, :L, :] = jnp.zeros((L, G * HEAD_DIM), BF16)

    def body(mq, carry):
        r0 = pl.multiple_of(mq * tq, tq)
        q0 = pl.multiple_of(L + mq * tq, tq)
        cos = cos_ref[pl.ds(r0, tq), :]
        sa = sa_ref[pl.ds(r0, tq), :]
        sb = sb_ref[pl.ds(r0, tq), :]
        for hh in range(G):
            qn = _rms(q_ref[0, pl.ds(q0, tq), hh * HEAD_DIM:(hh + 1) * HEAD_DIM].astype(F32), qg)
            qh = _rope(qn, cos, sa, sb).astype(BF16)
            o = _attend_augmented(qh, kn_ref[...], va_ref[...])
            o_ref[0, pl.ds(q0, tq), hh * HEAD_DIM:(hh + 1) * HEAD_DIM] = o.astype(BF16)
        return carry

    lax.fori_loop(0, n_lat // tq, body, 0, unroll=4)


def _gqa_attention(qkv, rope, q_g, k_g, need_ctx):
    b, t, _ = qkv.shape
    n_lat = t - CTX_LEN
    G = GQA_HEADS // GQA_KV_HEADS
    gw = G * HEAD_DIM
    cos, sin_a, sin_b = rope
    kern = functools.partial(_gqa_kernel, need_ctx=need_ctx, n_lat=n_lat)
    tab_spec = pl.BlockSpec((n_lat, HEAD_DIM), lambda bi, gi: (0, 0))
    vec_spec = pl.BlockSpec((1, HEAD_DIM), lambda bi, gi: (0, 0))
    return pl.pallas_call(
        kern,
        grid=(b, GQA_KV_HEADS),
        in_specs=[
            pl.BlockSpec((1, t, gw), lambda bi, gi: (bi, 0, gi)),
            pl.BlockSpec((1, t, HEAD_DIM), lambda bi, gi: (bi, 0, GQA_HEADS + gi)),
            pl.BlockSpec((1, t, HEAD_DIM), lambda bi, gi: (bi, 0, GQA_HEADS + GQA_KV_HEADS + gi)),
            tab_spec, tab_spec, tab_spec, vec_spec, vec_spec,
        ],
        out_specs=pl.BlockSpec((1, t, gw), lambda bi, gi: (bi, 0, gi)),
        out_shape=jax.ShapeDtypeStruct((b, t, GQA_HEADS * HEAD_DIM), BF16),
        scratch_shapes=[pltpu.VMEM((t, HEAD_DIM), BF16), pltpu.VMEM((t, 2 * HEAD_DIM), BF16)],
        compiler_params=_cparams(2),
        name="gqa_attn",
    )(qkv, qkv, qkv, cos, sin_a, sin_b, q_g.reshape(1, HEAD_DIM), k_g.reshape(1, HEAD_DIM))


def _diff_kernel(q_ref, k_ref, v_ref, cos_ref, sa_ref, sb_ref, lam_ref, g_ref, o_ref, kr_ref, *, need_ctx, n_lat, lambda_init):
    L = CTX_LEN
    scale = SCORE_SCALE
    tq = 128
    lam = lam_ref[...]
    lam_full = (jnp.exp(jnp.sum(lam[0:1] * lam[1:2], axis=-1, keepdims=True))
                - jnp.exp(jnp.sum(lam[2:3] * lam[3:4], axis=-1, keepdims=True)) + lambda_init)
    g = g_ref[...] * (1.0 - lambda_init)

    kr_ref[:L, :] = k_ref[0, :L, :]
    for c in range(2):
        k_lat = k_ref[0, L:, c * HEAD_DIM:(c + 1) * HEAD_DIM].astype(F32)
        kr_ref[L:, c * HEAD_DIM:(c + 1) * HEAD_DIM] = _rope(k_lat, cos_ref[...], sa_ref[...], sb_ref[...]).astype(BF16)

    def attend(qs, n_keys):
        ps = []
        for c in range(2):
            s = lax.dot_general(qs[c], kr_ref[:n_keys, c * HEAD_DIM:(c + 1) * HEAD_DIM], _NT, preferred_element_type=F32)
            p, l = _softmax_parts(s)
            ps.append(p * ((1.0 if c == 0 else lam_full) / l))
        a = (ps[0] - ps[1]).astype(BF16)
        o = jnp.dot(a, v_ref[0, :n_keys, :], preferred_element_type=F32)
        return _rms(o, g).astype(BF16)

    if need_ctx:
        qs = [(q_ref[0, :L, c * HEAD_DIM:(c + 1) * HEAD_DIM].astype(F32) * scale).astype(BF16) for c in range(2)]
        o_ref[0, :L, :] = attend(qs, L)
    else:
        o_ref[0, :L, :] = jnp.zeros((L, 2 * HEAD_DIM), BF16)

    def body(mq, carry):
        r0 = pl.multiple_of(mq * tq, tq)
        q0 = pl.multiple_of(L + mq * tq, tq)
        cos = cos_ref[pl.ds(r0, tq), :]
        sa = sa_ref[pl.ds(r0, tq), :]
        sb = sb_ref[pl.ds(r0, tq), :]
        qs = []
        for c in range(2):
            q = q_ref[0, pl.ds(q0, tq), c * HEAD_DIM:(c + 1) * HEAD_DIM].astype(F32) * scale
            qs.append(_rope(q, cos, sa, sb).astype(BF16))
        o_ref[0, pl.ds(q0, tq), :] = attend(qs, L + n_lat)
        return carry

    lax.fori_loop(0, n_lat // tq, body, 0, unroll=4)


def _diff_attention(qkv, rope, lam, subln_g, layer_idx, need_ctx):
    b, t, _ = qkv.shape
    n_lat = t - CTX_LEN
    hw = 2 * HEAD_DIM
    cos, sin_a, sin_b = rope
    lambda_init = 0.8 - 0.6 * math.exp(-0.3 * layer_idx)
    kern = functools.partial(_diff_kernel, need_ctx=need_ctx, n_lat=n_lat, lambda_init=lambda_init)
    tab_spec = pl.BlockSpec((n_lat, HEAD_DIM), lambda bi, hi: (0, 0))
    return pl.pallas_call(
        kern,
        grid=(b, DIFF_HEADS),
        in_specs=[
            pl.BlockSpec((1, t, hw), lambda bi, hi: (bi, 0, hi)),
            pl.BlockSpec((1, t, hw), lambda bi, hi: (bi, 0, DIFF_HEADS + hi)),
            pl.BlockSpec((1, t, hw), lambda bi, hi: (bi, 0, 2 * DIFF_HEADS + hi)),
            tab_spec, tab_spec, tab_spec,
            pl.BlockSpec((4, HEAD_DIM), lambda bi, hi: (0, 0)),
            pl.BlockSpec((1, hw), lambda bi, hi: (0, 0)),
        ],
        out_specs=pl.BlockSpec((1, t, hw), lambda bi, hi: (bi, 0, hi)),
        out_shape=jax.ShapeDtypeStruct((b, t, DIFF_HEADS * hw), BF16),
        scratch_shapes=[pltpu.VMEM((t, hw), BF16)],
        compiler_params=_cparams(2),
        name="diff_attn",
    )(qkv, qkv, qkv, cos, sin_a, sin_b, lam, subln_g.reshape(1, hw))


def _router_kernel(x_ref, g_ref, sh_ref, sc_ref, rw_ref, rb_ref, h_ref, e_ref, r_ref, gt_ref, cnt_ref, run_ref, *, tpb, ctx_row):
    i = pl.program_id(0)

    @pl.when(i == 0)
    def _():
        run_ref[...] = jnp.zeros_like(run_ref)

    y = _rms(x_ref[...], g_ref[...])
    h = y * (1.0 + _row_mod(i, tpb, ctx_row, sc_ref)) + _row_mod(i, tpb, ctx_row, sh_ref)
    h_ref[...] = _to_row_tiles(_pack_bf16_pairs(h))

    h_hi = h.astype(BF16)
    h_lo = (h - h_hi.astype(F32)).astype(BF16)
    w = rw_ref[...]
    w_hi = w.astype(BF16)
    w_lo = (w - w_hi.astype(F32)).astype(BF16)
    logits = (jnp.dot(h_hi, w_hi, preferred_element_type=F32) + jnp.dot(h_lo, w_hi, preferred_element_type=F32)
              + jnp.dot(h_hi, w_lo, preferred_element_type=F32)) + rb_ref[...]

    lane = lax.broadcasted_iota(jnp.int32, (ROW_TILE, LANES), 1)
    lane_f = lane.astype(F32)
    work = jnp.where(lane < N_EXPERTS, logits, -jnp.inf)
    vals, onehots = [], []
    for _ in range(TOP_K):
        m = jnp.max(work, axis=-1, keepdims=True)
        idx = jnp.min(jnp.where(work == m, lane_f, float(LANES)), axis=-1, keepdims=True)
        hit = lane_f == idx
        vals.append(m)
        onehots.append(hit)
        work = jnp.where(hit, -jnp.inf, work)

    exps = [jnp.exp(v - vals[0]) for v in vals]
    denom = exps[0] + exps[1] + exps[2] + exps[3]

    member = onehots[0]
    for k in range(1, TOP_K):
        member = jnp.logical_or(member, onehots[k])
    member_f = member.astype(F32)
    r_i = lax.broadcasted_iota(jnp.int32, (ROW_TILE, ROW_TILE), 0)
    c_i = lax.broadcasted_iota(jnp.int32, (ROW_TILE, ROW_TILE), 1)
    strict_lower = (c_i < r_i).astype(BF16)
    before = jnp.dot(strict_lower, member_f.astype(BF16), preferred_element_type=F32) + run_ref[...]

    e_out = jnp.zeros((ROW_TILE, LANES), F32)
    r_out = jnp.zeros((ROW_TILE, LANES), F32)
    g_out = jnp.zeros((ROW_TILE, LANES), F32)
    for k in range(TOP_K):
        hit = onehots[k]
        e_k = jnp.sum(jnp.where(hit, lane_f, 0.0), axis=-1, keepdims=True)
        r_k = jnp.sum(jnp.where(hit, before, 0.0), axis=-1, keepdims=True)
        e_out = jnp.where(lane == k, e_k, e_out)
        r_out = jnp.where(lane == k, r_k, r_out)
        g_out = jnp.where(lane == k, exps[k] / denom, g_out)
    e_ref[...] = e_out.astype(jnp.int32)
    r_ref[...] = r_out.astype(jnp.int32)
    gt_ref[...] = g_out

    run_ref[...] = run_ref[...] + jnp.sum(member_f, axis=0, keepdims=True)
    cnt_ref[...] = run_ref[...].astype(jnp.int32)


def _router(xs, g, mod, router_w, router_b, tpb, ctx_row):
    n_tok, d = xs.shape
    rw = jnp.zeros((d, LANES), F32).at[:, :N_EXPERTS].set(router_w)
    rb = jnp.zeros((1, LANES), F32).at[0, :N_EXPERTS].set(router_b)
    kern = functools.partial(_router_kernel, tpb=tpb, ctx_row=ctx_row)
    row_spec = pl.BlockSpec((ROW_TILE, LANES), lambda i: (i, 0))
    return pl.pallas_call(
        kern,
        grid=(n_tok // ROW_TILE,),
        in_specs=[
            pl.BlockSpec((ROW_TILE, d), lambda i: (i, 0)),
            pl.BlockSpec((1, d), lambda i: (0, 0)),
            pl.BlockSpec((MOD_ROWS, d), lambda i: (0, 3)),
            pl.BlockSpec((MOD_ROWS, d), lambda i: (0, 4)),
            pl.BlockSpec((d, LANES), lambda i: (0, 0)),
            pl.BlockSpec((1, LANES), lambda i: (0, 0)),
        ],
        out_specs=[
            pl.BlockSpec((ROW_TILE, *SLOT_ROW), lambda i: (i, 0, 0)),
            row_spec, row_spec, row_spec,
            pl.BlockSpec((1, LANES), lambda i: (0, 0)),
        ],
        out_shape=[
            jax.ShapeDtypeStruct((n_tok, *SLOT_ROW), jnp.int32),
            jax.ShapeDtypeStruct((n_tok, LANES), jnp.int32),
            jax.ShapeDtypeStruct((n_tok, LANES), jnp.int32),
            jax.ShapeDtypeStruct((n_tok, LANES), F32),
            jax.ShapeDtypeStruct((1, LANES), jnp.int32),
        ],
        scratch_shapes=[pltpu.VMEM((1, LANES), F32)],
        compiler_params=_cparams(1),
        name="moe_router",
    )(xs, g.reshape(1, d), mod, mod, rw, rb)


DISPATCH_TILE = 512


def _dispatch_kernel(zero_blocks_ref, dest_ref, h_ref, slots_ref, zero_ref, sem, zero_sem):
    @pl.when(pl.program_id(0) == 0)
    def _():
        zero_ref[...] = jnp.zeros_like(zero_ref)

        def zero_copy(j):
            rows = pl.ds(pl.multiple_of(zero_blocks_ref[j] * EXPERT_TILE, EXPERT_TILE), EXPERT_TILE)
            return pltpu.make_async_copy(zero_ref, slots_ref.at[rows], zero_sem)

        def for_each_zero_block(fn):
            def body(j, carry):
                @pl.when(zero_blocks_ref[j] >= 0)
                def _():
                    fn(j)
                return carry
            lax.fori_loop(0, zero_blocks_ref.shape[0], body, 0)

        for_each_zero_block(lambda j: zero_copy(j).start())
        for_each_zero_block(lambda j: zero_copy(j).wait())

    def row_copy(t, k):
        d = dest_ref[t * TOP_K + k]
        return pltpu.make_async_copy(h_ref.at[t], slots_ref.at[d], sem)

    def start(t, carry):
        for k in range(TOP_K):
            row_copy(t, k).start(priority=k % 2)
        return carry

    def wait(t, carry):
        for k in range(TOP_K):
            row_copy(t, k).wait()
        return carry

    lax.fori_loop(0, DISPATCH_TILE, start, 0)
    lax.fori_loop(0, DISPATCH_TILE, wait, 0)


def _dispatch(h_packed, dest_flat, zero_blocks, n_slots):
    n_tok = h_packed.shape[0]
    grid_spec = pltpu.PrefetchScalarGridSpec(
        num_scalar_prefetch=1,
        grid=(n_tok // DISPATCH_TILE,),
        in_specs=[
            pl.BlockSpec((DISPATCH_TILE * TOP_K,), lambda i, zb: (i,), memory_space=pltpu.SMEM),
            pl.BlockSpec((DISPATCH_TILE, *SLOT_ROW), lambda i, zb: (i, 0, 0)),
        ],
        out_specs=pl.BlockSpec(memory_space=pl.ANY),
        scratch_shapes=[pltpu.VMEM((EXPERT_TILE, *SLOT_ROW), jnp.int32), pltpu.SemaphoreType.DMA(()), pltpu.SemaphoreType.DMA(())],
    )
    return pl.pallas_call(
        _dispatch_kernel,
        grid_spec=grid_spec,
        out_shape=jax.ShapeDtypeStruct((n_slots, *SLOT_ROW), jnp.int32),
        compiler_params=_cparams(1),
        name="moe_dispatch",
    )(zero_blocks, dest_flat, h_packed)


CAST_ROWS = 256


def _expert_kernel(be_ref, nx_ref, nu_ref, x_ref, bg_ref, bu_ref, bd_ref, wg_hbm, wu_hbm, wd_hbm, y_ref,
                   stage_g, stage_u, stage_d, wg_bf, wu_bf, wd_bf, sem, *, layer):
    b = pl.program_id(0)
    used = b < nu_ref[0]
    e = be_ref[b]
    first = jnp.logical_or(b == 0, be_ref[jnp.maximum(b - 1, 0)] != e)

    def weight_copies(ex):
        return (pltpu.make_async_copy(wg_hbm.at[layer, ex], stage_g, sem.at[0]),
                pltpu.make_async_copy(wu_hbm.at[layer, ex], stage_u, sem.at[1]),
                pltpu.make_async_copy(wd_hbm.at[layer, ex], stage_d, sem.at[2]))

    @pl.when(b == 0)
    def _():
        for cp in weight_copies(e):
            cp.start()

    @pl.when(jnp.logical_and(used, first))
    def _():
        for cp in weight_copies(e):
            cp.wait()

        def cast(src, dst):
            def step(r, carry):
                rows = pl.ds(pl.multiple_of(r * CAST_ROWS, CAST_ROWS), CAST_ROWS)
                dst[rows, :] = src[rows, :].astype(BF16)
                return carry
            lax.fori_loop(0, src.shape[0] // CAST_ROWS, step, 0)

        cast(stage_g, wg_bf)
        cast(stage_u, wu_bf)
        cast(stage_d, wd_bf)

        @pl.when(nx_ref[b] >= 0)
        def _():
            for cp in weight_copies(nx_ref[b]):
                cp.start()

    @pl.when(used)
    def _():
        x = _unpack_bf16_pairs(_from_row_tiles(x_ref[...])).astype(BF16)
        g = jnp.minimum(jnp.dot(x, wg_bf[...], preferred_element_type=F32) + bg_ref[0, 0], SWIGLU_LIMIT)
        u = jnp.clip(jnp.dot(x, wu_bf[...], preferred_element_type=F32) + bu_ref[0, 0], -SWIGLU_LIMIT, SWIGLU_LIMIT)
        act = (u + 1.0) * (g * jax.nn.sigmoid(SWIGLU_ALPHA * g))
        y = jnp.dot(act.astype(BF16), wd_bf[...], preferred_element_type=F32) + bd_ref[0, 0]
        y_ref[...] = _to_row_tiles(_pack_bf16_pairs(y))

    @pl.when(jnp.logical_not(used))
    def _():
        y_ref[...] = jnp.zeros_like(y_ref)


def _experts(x_slots, block_expert, next_expert, n_used, layer, wg, bg, wu, bu, wd, bd):
    n_slots = x_slots.shape[0]
    n_blocks = n_slots // EXPERT_TILE
    depth, n_exp, d, de = wg.shape

    def used_block(b, be, nx, nu):
        return (jnp.minimum(b, nu[0] - 1), 0, 0)

    def bias_spec(width):
        return pl.BlockSpec((1, 1, 1, width), lambda b, be, nx, nu: (layer, be[b], 0, 0))

    hbm = pl.BlockSpec(memory_space=pl.ANY)
    grid_spec = pltpu.PrefetchScalarGridSpec(
        num_scalar_prefetch=3,
        grid=(n_blocks,),
        in_specs=[pl.BlockSpec((EXPERT_TILE, *SLOT_ROW), used_block), bias_spec(de), bias_spec(de), bias_spec(d), hbm, hbm, hbm],
        out_specs=pl.BlockSpec((EXPERT_TILE, *SLOT_ROW), lambda b, be, nx, nu: (b, 0, 0)),
        scratch_shapes=[
            pltpu.VMEM((d, de), F32), pltpu.VMEM((d, de), F32), pltpu.VMEM((de, d), F32),
            pltpu.VMEM((d, de), BF16), pltpu.VMEM((d, de), BF16), pltpu.VMEM((de, d), BF16),
            pltpu.SemaphoreType.DMA((3,)),
        ],
    )
    return pl.pallas_call(
        functools.partial(_expert_kernel, layer=layer),
        grid_spec=grid_spec,
        out_shape=jax.ShapeDtypeStruct((n_slots, *SLOT_ROW), jnp.int32),
        compiler_params=_cparams(1),
        name="moe_experts",
    )(block_expert, next_expert, n_used, x_slots,
      bg.reshape(depth, n_exp, 1, de), bu.reshape(depth, n_exp, 1, de), bd.reshape(depth, n_exp, 1, d), wg, wu, wd)


COMBINE_TILE = 256


def _combine_kernel(dest_ref, dest_next_ref, y_hbm, gt_ref, x_ref, gate_ref, o_ref, buf, sem, *, tiles_per_batch, ctx_tiles, ctx_row):
    i = pl.program_id(0)
    n = pl.num_programs(0)
    slot = i % 2

    def row_copy(dref, s, t, k):
        return pltpu.make_async_copy(y_hbm.at[dref[t * TOP_K + k]], buf.at[s, k, t], sem.at[s])

    def start_tile(dref, s):
        def body(t, carry):
            for k in range(TOP_K):
                row_copy(dref, s, t, k).start(priority=k % 2)
            return carry
        lax.fori_loop(0, COMBINE_TILE, body, 0)

    @pl.when(i == 0)
    def _():
        start_tile(dest_ref, 0)

    @pl.when(i + 1 < n)
    def _():
        start_tile(dest_next_ref, 1 - slot)

    def wait_body(t, carry):
        for k in range(TOP_K):
            row_copy(dest_ref, slot, t, k).wait()
        return carry
    lax.fori_loop(0, COMBINE_TILE, wait_body, 0)

    gates = gt_ref[...]
    y = gates[:, 0:1] * _unpack_bf16_pairs(_from_row_tiles(buf[slot, 0]))
    for k in range(1, TOP_K):
        y = y + gates[:, k:k + 1] * _unpack_bf16_pairs(_from_row_tiles(buf[slot, k]))
    b = i // tiles_per_batch
    is_ctx = (i % tiles_per_batch) < ctx_tiles
    mod_gate = jnp.where(is_ctx, gate_ref[ctx_row:ctx_row + 1, :], gate_ref[pl.ds(b, 1), :])
    o_ref[...] = x_ref[...] + mod_gate * y


def _combine(y_slots, dest_flat, gates, xs, mod, rows_per_batch, ctx_row):
    n_tok, d = xs.shape
    n_tiles = n_tok // COMBINE_TILE
    kern = functools.partial(_combine_kernel, tiles_per_batch=rows_per_batch // COMBINE_TILE,
                             ctx_tiles=CTX_LEN // COMBINE_TILE, ctx_row=ctx_row)
    return pl.pallas_call(
        kern,
        grid=(n_tiles,),
        in_specs=[
            pl.BlockSpec((COMBINE_TILE * TOP_K,), lambda i: (i,), memory_space=pltpu.SMEM),
            pl.BlockSpec((COMBINE_TILE * TOP_K,), lambda i: (jnp.minimum(i + 1, n_tiles - 1),), memory_space=pltpu.SMEM),
            pl.BlockSpec(memory_space=pl.ANY),
            pl.BlockSpec((COMBINE_TILE, LANES), lambda i: (i, 0)),
            pl.BlockSpec((COMBINE_TILE, d), lambda i: (i, 0)),
            pl.BlockSpec((MOD_ROWS, d), lambda i: (0, 5)),
        ],
        out_specs=pl.BlockSpec((COMBINE_TILE, d), lambda i: (i, 0)),
        out_shape=jax.ShapeDtypeStruct((n_tok, d), F32),
        scratch_shapes=[pltpu.VMEM((2, TOP_K, COMBINE_TILE, *SLOT_ROW), jnp.int32), pltpu.SemaphoreType.DMA((2,))],
        compiler_params=_cparams(1),
        name="moe_combine",
    )(dest_flat, dest_flat, y_slots, gates, xs, mod)


def _final_norm_kernel(x_ref, g_ref, o_ref):
    o_ref[0] = _rms(x_ref[0], g_ref[...])


def _final_norm(xs3, g):
    b, t, d = xs3.shape
    blk = CTX_LEN
    n_lat = t - CTX_LEN
    return pl.pallas_call(
        _final_norm_kernel,
        grid=(b, n_lat // blk),
        in_specs=[
            pl.BlockSpec((1, blk, d), lambda bi, j: (bi, j + 1, 0)),
            pl.BlockSpec((1, d), lambda bi, j: (0, 0)),
        ],
        out_specs=pl.BlockSpec((1, blk, d), lambda bi, j: (bi, j, 0)),
        out_shape=jax.ShapeDtypeStruct((b, n_lat, d), F32),
        compiler_params=_cparams(2),
        name="final_norm",
    )(xs3, g.reshape(1, d))


def _moe_layer(xs, g, mod, rows_per_batch, ctx_row, layer, router_w, router_b, wg, bg, wu, bu, wd, bd):
    n_tok, d = xs.shape
    tpb = rows_per_batch // ROW_TILE
    h, e_pad, r_pad, gates, cnt = _router(xs, g, mod, router_w, router_b, tpb, ctx_row)
    counts = cnt[0, :N_EXPERTS]
    padded = ((counts + EXPERT_TILE - 1) // EXPERT_TILE) * EXPERT_TILE
    pend = jnp.cumsum(padded)
    pstart = pend - padded
    n_blocks = (n_tok * TOP_K) // EXPERT_TILE + N_EXPERTS
    experts = jnp.arange(N_EXPERTS, dtype=jnp.int32)
    e_idx = e_pad[:, :TOP_K]
    dest = (jnp.sum(jnp.where(e_idx[..., None] == experts, pstart, 0), axis=-1) + r_pad[:, :TOP_K]).reshape(-1).astype(jnp.int32)
    block_row0 = jnp.arange(n_blocks, dtype=jnp.int32) * EXPERT_TILE
    block_expert = jnp.minimum(jnp.sum(pend[None, :] <= block_row0[:, None], axis=1), N_EXPERTS - 1).astype(jnp.int32)
    n_used = (pend[-1:] // EXPERT_TILE).astype(jnp.int32)
    later = jnp.logical_and(block_expert[None, :] > block_expert[:, None], (block_row0 < pend[-1])[None, :])
    next_expert = jnp.min(jnp.where(later, block_expert[None, :], N_EXPERTS), axis=1)
    next_expert = jnp.where(next_expert == N_EXPERTS, -1, next_expert).astype(jnp.int32)
    last_block = jnp.where(padded > counts, pend // EXPERT_TILE - 1, -1)
    tail_block = n_used[0] + experts
    zero_blocks = jnp.concatenate([last_block, jnp.where(tail_block < n_blocks, tail_block, -1)]).astype(jnp.int32)
    x_slots = _dispatch(h, dest, zero_blocks, n_blocks * EXPERT_TILE)
    y_slots = _experts(x_slots, block_expert, next_expert, n_used, layer, wg, bg, wu, bu, wd, bd)
    return _combine(y_slots, dest, gates, xs, mod, rows_per_batch, ctx_row)


def kernel(x, c, ctx, c_ctx, ada_w, ada_b, norm_mix_g, norm_ffn_g, final_g, na_wqkv, na_wo, na_rpb, gqa_wqkv, gqa_wo, gqa_q_g, gqa_k_g, diff_wqkv, diff_wo, diff_lambda, diff_subln_g, router_w, router_b, w_gate, b_gate, w_up, b_up, w_down, b_down):
    B, S, D = x.shape
    L = ctx.shape[1]
    T = L + S
    assert D == D_MODEL and L == CTX_LEN and T % ROW_TILE == 0 and B < MOD_ROWS
    assert (B * T) % DISPATCH_TILE == 0 and T % COMBINE_TILE == 0 and L % COMBINE_TILE == 0
    tpb = T // ROW_TILE
    ctx_row = B

    cvec = jnp.zeros((MOD_ROWS, D), F32).at[:B].set(c).at[B].set(c_ctx)
    mods = _ada_mod(cvec, ada_w, ada_b)
    rope = _rope_tables(S)

    xs = jnp.concatenate([ctx, x], axis=1).reshape(B * T, D)
    for i in range(DEPTH):
        need_ctx = i < DEPTH - 1
        mod = mods[i]
        kind, j = i % 3, i // 3
        if kind == 0:
            q_cols = jnp.arange(3 * D) < D
            w_qkv = (na_wqkv[j] * jnp.where(q_cols, SCORE_SCALE, 1.0)).astype(BF16)
            qkv = _norm_mm(xs, norm_mix_g[i], mod, w_qkv, tpb, ctx_row).reshape(B, T, -1)
            o = _na_attention(qkv, _na_bias_table(na_rpb[j]), need_ctx)
            w_o = na_wo[j]
        elif kind == 1:
            qkv = _norm_mm(xs, norm_mix_g[i], mod, gqa_wqkv[j].astype(BF16), tpb, ctx_row).reshape(B, T, -1)
            o = _gqa_attention(qkv, rope, gqa_q_g[j], gqa_k_g[j], need_ctx)
            w_o = gqa_wo[j]
        else:
            qkv = _norm_mm(xs, norm_mix_g[i], mod, diff_wqkv[j].astype(BF16), tpb, ctx_row).reshape(B, T, -1)
            o = _diff_attention(qkv, rope, diff_lambda[j], diff_subln_g[j], i, need_ctx)
            w_o = diff_wo[j]
        xs = _out_proj(o.reshape(B * T, D), w_o.astype(BF16), xs, mod, tpb, ctx_row)
        xs = _moe_layer(xs, norm_ffn_g[i], mod, T, ctx_row, i, router_w[i], router_b[i],
                        w_gate, b_gate, w_up, b_up, w_down, b_down)
    return _final_norm(xs.reshape(B, T, D), final_g)
```
